```python
import math
import jax, jax.numpy as jnp
from jax import lax
import numpy as np

D_MODEL = 2048
BATCH = 1
SEQ = 16384
DEPTH = 1
DEC_BATCH = 2
DEC_SEQ = 8192
PAST_LEN = 128

D_MIX = D_MODEL
D_HYENA = D_MIX // 2
D_FNET = D_MIX - D_HYENA
HYENA_HEADS = 8
HYENA_HEAD_DIM = D_HYENA // HYENA_HEADS
FNET_GROUPS = 4
FNET_GROUP_DIM = D_FNET // FNET_GROUPS
HYENA_ORDER = 2
N_DIR = 2
SHORT_CONV = 3
FILTER_EMB = 33
FILTER_HIDDEN = 64
DECAY_TARGET = 1e-2
FAST_DECAY_PCT = 0.3
SLOW_DECAY_PCT = 1.5
MIN_DECAY = math.log(DECAY_TARGET) / SLOW_DECAY_PCT
MAX_DECAY = math.log(DECAY_TARGET) / FAST_DECAY_PCT
EPS = 1e-6
D_PROJ = 3 * D_HYENA + D_HYENA + D_FNET + D_FNET

kernel_name = "hymba_hyena_fnet_bidir_encoder"


def rmsnorm(x, g):
    xf = x.astype(jnp.float32)
    y = xf * lax.rsqrt(jnp.mean(xf * xf, axis=-1, keepdims=True) + EPS)
    return y.astype(x.dtype) * g


def positional_features(L):
    bands = (FILTER_EMB - 1) // 2
    t = jnp.linspace(0.0, 1.0, L, dtype=jnp.float32)
    w = 2.0 * math.pi * jnp.arange(L, dtype=jnp.float32) / L
    f = jnp.linspace(1e-4, bands - 1, bands, dtype=jnp.float32)
    fw = w[:, None] * f[None, :]
    return jnp.concatenate([t[:, None], jnp.cos(fw), -jnp.sin(fw)], axis=-1)


def hyena_filters(L, w1, b1, w2, b2, w3, b3, w4, freq, dtype):
    z = positional_features(L).astype(dtype)
    h = jnp.sin(freq * (z @ w1 + b1))
    h = jnp.sin(freq * (h @ w2 + b2))
    h = jnp.sin(freq * (h @ w3 + b3))
    h = (h @ w4).astype(jnp.float32).reshape(L, HYENA_ORDER, N_DIR, D_HYENA)
    t = jnp.linspace(0.0, 1.0, L, dtype=jnp.float32)[:, None]
    deltas = jnp.tile(jnp.linspace(MIN_DECAY, MAX_DECAY, HYENA_HEAD_DIM, dtype=jnp.float32), HYENA_HEADS)
    decay = jnp.exp(-t * jnp.abs(deltas)[None, :])
    return h * decay[:, None, None, :]


def bidir_long_conv(v, h_fwd, h_bwd, skip):
    L = v.shape[1]
    n = 2 * L
    k = jnp.concatenate([h_fwd, jnp.zeros((1, h_fwd.shape[1]), jnp.float32), h_bwd[:0:-1]], axis=0)
    k_f = jnp.fft.rfft(k, n=n, axis=0)
    vf = v.astype(jnp.float32)
    v_f = jnp.fft.rfft(vf, n=n, axis=1)
    y = jnp.fft.irfft(v_f * k_f[None], n=n, axis=1)[:, :L]
    return (y + vf * skip.astype(jnp.float32)).astype(v.dtype)


def centred_short_conv(u, w, b):
    L = u.shape[1]
    up = jnp.pad(u, ((0, 0), (1, 1), (0, 0)))
    return up[:, :L] * w[0] + up[:, 1:L + 1] * w[1] + up[:, 2:] * w[2] + b


def mixer_layer(x, norm_g, w_in, conv_w, conv_b, filt_w1, filt_b1, filt_w2, filt_b2,
                filt_w3, filt_b3, filt_w4, filt_freq, filt_skip, fnet_w, fnet_b,
                norm_hy, norm_fn, w_out):
    B, L, _ = x.shape
    h = rmsnorm(x, norm_g)
    proj = h @ w_in
    u_hy = proj[..., :3 * D_HYENA]
    z_hy = proj[..., 3 * D_HYENA:4 * D_HYENA]
    u_fn = proj[..., 4 * D_HYENA:4 * D_HYENA + D_FNET]
    z_fn = proj[..., 4 * D_HYENA + D_FNET:]

    u_hy = centred_short_conv(u_hy, conv_w, conv_b)
    x1 = u_hy[..., :D_HYENA]
    x2 = u_hy[..., D_HYENA:2 * D_HYENA]
    v = u_hy[..., 2 * D_HYENA:]
    filt = hyena_filters(L, filt_w1, filt_b1, filt_w2, filt_b2, filt_w3, filt_b3,
                         filt_w4, filt_freq, x.dtype)
    zz = x1 * bidir_long_conv(v, filt[:, 0, 0], filt[:, 0, 1], filt_skip[0])
    zz = x2 * bidir_long_conv(zz, filt[:, 1, 0], filt[:, 1, 1], filt_skip[1])
    y_hy = rmsnorm(zz * jax.nn.silu(z_hy), norm_hy)

    ug = u_fn.astype(jnp.float32).reshape(B, L, FNET_GROUPS, FNET_GROUP_DIM)
    fg = jnp.fft.fft2(ug, axes=(1, 3), norm="ortho").real.astype(x.dtype)
    yg = jnp.einsum("blgc,gcd->blgd", fg, fnet_w) + fnet_b.reshape(FNET_GROUPS, FNET_GROUP_DIM)
    y_fn = rmsnorm(yg.reshape(B, L, D_FNET) * jax.nn.silu(z_fn), norm_fn)

    y = jnp.concatenate([y_hy, y_fn], axis=-1) @ w_out
    return x + y


def encoder_trunk(x, norm_g, w_in, conv_w, conv_b, filt_w1, filt_b1, filt_w2, filt_b2,
                  filt_w3, filt_b3, filt_w4, filt_freq, filt_skip, fnet_w, fnet_b,
                  norm_hy, norm_fn, w_out, final_norm):
    for l in range(DEPTH):
        x = mixer_layer(x, norm_g[l], w_in[l], conv_w[l], conv_b[l], filt_w1[l], filt_b1[l],
                        filt_w2[l], filt_b2[l], filt_w3[l], filt_b3[l], filt_w4[l],
                        filt_freq[l], filt_skip[l], fnet_w[l], fnet_b[l],
                        norm_hy[l], norm_fn[l], w_out[l])
    return rmsnorm(x, final_norm)


def setup_inputs(seed: int = 0) -> dict:
    key = jax.random.key(seed)
    ks = jax.random.split(key, 24)
    f32 = jnp.float32
    nrm = lambda k, shape, s: (jax.random.normal(k, shape, f32) * s)
    return {
        "x_prompt": nrm(ks[0], (BATCH, SEQ, D_MODEL), 1.0),
        "x_sample": nrm(ks[1], (DEC_BATCH, DEC_SEQ, D_MODEL), 1.0),
        "norm_g": 1.0 + nrm(ks[2], (DEPTH, D_MODEL), 0.01),
        "w_in": nrm(ks[3], (DEPTH, D_MODEL, D_PROJ), D_MODEL ** -0.5),
        "conv_w": nrm(ks[4], (DEPTH, SHORT_CONV, 3 * D_HYENA), SHORT_CONV ** -0.5),
        "conv_b": nrm(ks[5], (DEPTH, 3 * D_HYENA), 0.01),
        "filt_w1": nrm(ks[6], (DEPTH, FILTER_EMB, FILTER_HIDDEN), FILTER_EMB ** -0.5),
        "filt_b1": nrm(ks[7], (DEPTH, FILTER_HIDDEN), 0.01),
        "filt_w2": nrm(ks[8], (DEPTH, FILTER_HIDDEN, FILTER_HIDDEN), FILTER_HIDDEN ** -0.5),
        "filt_b2": nrm(ks[9], (DEPTH, FILTER_HIDDEN), 0.01),
        "filt_w3": nrm(ks[10], (DEPTH, FILTER_HIDDEN, FILTER_HIDDEN), FILTER_HIDDEN ** -0.5),
        "filt_b3": nrm(ks[11], (DEPTH, FILTER_HIDDEN), 0.01),
        "filt_w4": nrm(ks[12], (DEPTH, FILTER_HIDDEN, HYENA_ORDER * N_DIR * D_HYENA), 0.05 * FILTER_HIDDEN ** -0.5),
        "filt_freq": 1.0 + nrm(ks[13], (DEPTH, FILTER_HIDDEN), 0.01),
        "filt_skip": nrm(ks[14], (DEPTH, HYENA_ORDER, D_HYENA), 1.0),
        "fnet_w": nrm(ks[15], (DEPTH, FNET_GROUPS, FNET_GROUP_DIM, FNET_GROUP_DIM), FNET_GROUP_DIM ** -0.5),
        "fnet_b": nrm(ks[16], (DEPTH, D_FNET), 0.01),
        "norm_hy": 1.0 + nrm(ks[17], (DEPTH, D_HYENA), 0.01),
        "norm_fn": 1.0 + nrm(ks[18], (DEPTH, D_FNET), 0.01),
        "w_out": nrm(ks[19], (DEPTH, D_MIX, D_MODEL), D_MIX ** -0.5),
        "final_norm": 1.0 + nrm(ks[20], (D_MODEL,), 0.01),
    }


def reference(x_prompt, x_sample, norm_g, w_in, conv_w, conv_b, filt_w1, filt_b1, filt_w2,
              filt_b2, filt_w3, filt_b3, filt_w4, filt_freq, filt_skip, fnet_w, fnet_b,
              norm_hy, norm_fn, w_out, final_norm):
    y_prompt = encoder_trunk(x_prompt, norm_g, w_in, conv_w, conv_b, filt_w1, filt_b1, filt_w2,
                             filt_b2, filt_w3, filt_b3, filt_w4, filt_freq, filt_skip, fnet_w,
                             fnet_b, norm_hy, norm_fn, w_out, final_norm)
    y_sample = encoder_trunk(x_sample, norm_g, w_in, conv_w, conv_b, filt_w1, filt_b1, filt_w2,
                             filt_b2, filt_w3, filt_b3, filt_w4, filt_freq, filt_skip, fnet_w,
                             fnet_b, norm_hy, norm_fn, w_out, final_norm)
    return (y_prompt, y_sample)
```

```python
import functools
import math

import numpy as np
import jax
import jax.numpy as jnp
from jax import lax
from jax.experimental import pallas as pl
from jax.experimental.pallas import tpu as pltpu

F32 = jnp.float32
BF16 = jnp.bfloat16

D_MODEL = 2048
D_HYENA = D_MODEL // 2
D_FNET = D_MODEL - D_HYENA
HYENA_HEADS = 8
HYENA_HEAD_DIM = D_HYENA // HYENA_HEADS
FNET_GROUPS = 4
FNET_GROUP_DIM = D_FNET // FNET_GROUPS
FILTER_EMB = 33
FILTER_HIDDEN = 64
MIN_DECAY = math.log(1e-2) / 1.5
MAX_DECAY = math.log(1e-2) / 0.3
EPS = 1e-6

N_PROJ_GROUPS = 6
CH = 1024
N2 = 128
HALO = 16
VMEM_LIMIT = 52 * 1024 * 1024


def _cparams(*sem):
    return pltpu.CompilerParams(dimension_semantics=sem, vmem_limit_bytes=VMEM_LIMIT)


def _conv_tables(L):
    n = 2 * L
    n1 = n // N2
    h = n1 // 2
    k1 = np.arange(h, dtype=np.float64)
    ang1 = 2.0 * np.pi * np.outer(k1 + 0.5, np.arange(h)) / n1
    f1 = np.concatenate([np.cos(ang1), -np.sin(ang1)], axis=0)
    f3 = (2.0 / n) * np.concatenate([np.cos(ang1).T, -np.sin(ang1).T], axis=1)
    angw = 2.0 * np.pi * np.outer(k1 + 0.5, np.arange(N2)) / n
    return (f1.astype(np.float32), f3.astype(np.float32),
            np.cos(angw).astype(np.float32), (-np.sin(angw)).astype(np.float32))


def _f2_tables():
    ang = 2.0 * np.pi * np.outer(np.arange(N2), np.arange(N2)) / N2
    return np.cos(ang).astype(np.float32), (-np.sin(ang)).astype(np.float32)


def _fnet_tables(L):
    n1 = L // N2
    ang1 = 2.0 * np.pi * np.outer(np.arange(n1), np.arange(n1)) / n1
    c, s = np.cos(ang1), np.sin(ang1)
    fn1 = np.block([[c, s], [-s, c]])
    angw = 2.0 * np.pi * np.outer(np.arange(n1), np.arange(N2)) / L
    scale = 1.0 / math.sqrt(L * FNET_GROUP_DIM)
    gd = FNET_GROUP_DIM
    angc = 2.0 * np.pi * np.outer(np.arange(gd), np.arange(gd)) / gd
    fc = np.concatenate([np.cos(angc), -np.sin(angc)], axis=1)
    return (fn1.astype(np.float32), (scale * np.cos(angw)).astype(np.float32),
            (-scale * np.sin(angw)).astype(np.float32), fc.astype(np.float32))


def _positional_features(L):
    bands = (FILTER_EMB - 1) // 2
    t = np.linspace(0.0, 1.0, L)
    w = 2.0 * np.pi * np.arange(L) / L
    f = np.linspace(1e-4, bands - 1, bands)
    fw = w[:, None] * f[None, :]
    z = np.concatenate([t[:, None], np.cos(fw), -np.sin(fw)], axis=-1)
    zp = np.zeros((L, FILTER_HIDDEN), np.float64)
    zp[:, :FILTER_EMB] = z
    return zp.astype(np.float32)


def _inproj_kernel(xm_ref, xt_ref, xb_ref, g_ref, w_ref, cw_ref, cb_ref, o_ref, hn_ref, *, tm, n_conv):
    i = pl.program_id(1)
    j = pl.program_id(2)
    last_i = pl.num_programs(1) - 1

    @pl.when(j == 0)
    def _():
        g = g_ref[...]

        def nrm(x):
            ms = jnp.mean(x * x, axis=-1, keepdims=True)
            return x * lax.rsqrt(ms + EPS) * g

        hn_ref[HALO:HALO + tm, :] = nrm(xm_ref[...]).astype(BF16)
        top = jnp.where(i > 0, nrm(xt_ref[...]), 0.0)
        bot = jnp.where(i < last_i, nrm(xb_ref[...]), 0.0)
        hn_ref[0:HALO, :] = top.astype(BF16)
        hn_ref[HALO + tm:, :] = bot.astype(BF16)

    @pl.when(j < n_conv)
    def _():
        rows = tm + 2 * HALO
        p = jnp.dot(hn_ref[...], w_ref[...], preferred_element_type=F32)
        prev = pltpu.roll(p, 1, axis=0)
        nxt = pltpu.roll(p, rows - 1, axis=0)
        cw = cw_ref[...]
        u = prev * cw[0:1, :] + p * cw[1:2, :] + nxt * cw[2:3, :] + cb_ref[...]
        o_ref[...] = u[HALO:HALO + tm, :]

    @pl.when(j >= n_conv)
    def _():
        o_ref[...] = jnp.dot(hn_ref[HALO:HALO + tm, :], w_ref[...], preferred_element_type=F32)


def _inproj(x, norm_g, w_in_bf, conv_w, conv_b, *, tm):
    b, L, d = x.shape
    n_conv = 3
    rt = tm // HALO
    nhalo = L // HALO
    kern = functools.partial(_inproj_kernel, tm=tm, n_conv=n_conv)
    return pl.pallas_call(
        kern,
        grid=(b, L // tm, N_PROJ_GROUPS),
        in_specs=[
            pl.BlockSpec((None, tm, d), lambda bb, i, j: (bb, i, 0)),
            pl.BlockSpec((None, HALO, d), lambda bb, i, j: (bb, jnp.maximum(i * rt - 1, 0), 0)),
            pl.BlockSpec((None, HALO, d), lambda bb, i, j: (bb, jnp.minimum((i + 1) * rt, nhalo - 1), 0)),
            pl.BlockSpec((1, d), lambda bb, i, j: (0, 0)),
            pl.BlockSpec((d, CH), lambda bb, i, j: (0, j)),
            pl.BlockSpec((3, CH), lambda bb, i, j: (0, jnp.minimum(j, n_conv - 1))),
            pl.BlockSpec((1, CH), lambda bb, i, j: (0, jnp.minimum(j, n_conv - 1))),
        ],
        out_specs=pl.BlockSpec((None, None, tm, CH), lambda bb, i, j: (bb, j, i, 0)),
        out_shape=jax.ShapeDtypeStruct((b, N_PROJ_GROUPS, L, CH), F32),
        scratch_shapes=[pltpu.VMEM((tm + 2 * HALO, d), BF16)],
        compiler_params=_cparams("arbitrary", "arbitrary", "arbitrary"),
        name="inproj",
    )(x, x, x, norm_g.reshape(1, d), w_in_bf, conv_w, conv_b.reshape(1, -1))


def _filter_kernel(z_ref, w1_ref, b1_ref, w2_ref, b2_ref, w3_ref, b3_ref, fr_ref, w4_ref, dl_ref,
                   o_ref, h_ref, *, tr, L):
    i = pl.program_id(0)
    j = pl.program_id(1)
    hp = lax.Precision.HIGHEST

    @pl.when(j == 0)
    def _():
        fr = fr_ref[...]
        h = jnp.sin(fr * (jnp.dot(z_ref[...], w1_ref[...], precision=hp, preferred_element_type=F32) + b1_ref[...]))
        h = jnp.sin(fr * (jnp.dot(h, w2_ref[...], precision=hp, preferred_element_type=F32) + b2_ref[...]))
        h = jnp.sin(fr * (jnp.dot(h, w3_ref[...], precision=hp, preferred_element_type=F32) + b3_ref[...]))
        h_ref[...] = h

    f = jnp.dot(h_ref[...], w4_ref[...], precision=hp, preferred_element_type=F32)
    row = i * tr + lax.broadcasted_iota(jnp.int32, (tr, 1), 0)
    t = row.astype(F32) * (1.0 / (L - 1))
    f = f * jnp.exp(-t * jnp.abs(dl_ref[...]))
    drop = jnp.logical_and(row == 0, (j % 2) == 1)
    o_ref[...] = jnp.where(drop, 0.0, f)


def _filters(L, w1p, b1, w2, b2, w3, b3, freq, w4, *, tr):
    z = jnp.asarray(_positional_features(L))
    deltas = np.tile(np.linspace(MIN_DECAY, MAX_DECAY, HYENA_HEAD_DIM), HYENA_HEADS).astype(np.float32)
    fh = FILTER_HIDDEN
    ncol = w4.shape[1] // CH
    vec = lambda: pl.BlockSpec((1, fh), lambda i, j: (0, 0))
    mat = lambda: pl.BlockSpec((fh, fh), lambda i, j: (0, 0))
    kern = functools.partial(_filter_kernel, tr=tr, L=L)
    return pl.pallas_call(
        kern,
        grid=(L // tr, ncol),
        in_specs=[
            pl.BlockSpec((tr, fh), lambda i, j: (i, 0)),
            mat(), vec(), mat(), vec(), mat(), vec(), vec(),
            pl.BlockSpec((fh, CH), lambda i, j: (0, j)),
            pl.BlockSpec((1, CH), lambda i, j: (0, 0)),
        ],
        out_specs=pl.BlockSpec((tr, CH), lambda i, j: (i, j)),
        out_shape=jax.ShapeDtypeStruct((L, ncol * CH), F32),
        scratch_shapes=[pltpu.VMEM((tr, fh), F32)],
        compiler_params=_cparams("arbitrary", "arbitrary"),
        name="filter_mlp",
    )(z, w1p, b1.reshape(1, fh), w2, b2.reshape(1, fh), w3, b3.reshape(1, fh), freq.reshape(1, fh),
      w4, jnp.asarray(deltas).reshape(1, CH))


def _lmm_kernel(m_ref, x_ref, o_ref):
    o_ref[...] = jnp.dot(m_ref[...], x_ref[...].astype(BF16),
                         preferred_element_type=F32).astype(o_ref.dtype)


def _lmm(mat_bf, x, x_index, out_dtype, *, tw, name):
    m, k = mat_bf.shape
    xb_shape = (None,) * (x.ndim - 2) + (k, tw)
    b = x.shape[0] if x.ndim > 2 else 1
    w = x.shape[-1]
    return pl.pallas_call(
        _lmm_kernel,
        grid=(b, w // tw),
        in_specs=[
            pl.BlockSpec((m, k), lambda bb, t: (0, 0)),
            pl.BlockSpec(xb_shape, x_index),
        ],
        out_specs=pl.BlockSpec((None, m, tw), lambda bb, t: (bb, 0, t)),
        out_shape=jax.ShapeDtypeStruct((b, m, w), out_dtype),
        compiler_params=_cparams("arbitrary", "arbitrary"),
        name=name,
    )(mat_bf, x)


def _inv1_kernel(m_ref, b_ref, gate_ref, v_ref, skip_ref, o_ref):
    y = jnp.dot(m_ref[...], b_ref[...], preferred_element_type=F32)
    o_ref[...] = gate_ref[...] * (y + skip_ref[...] * v_ref[...])


def _inv1(f3_bf, bspec, gate, gate_index, v, v_index, skip_tiled, *, tw):
    b, m2, w = bspec.shape
    h = f3_bf.shape[0]
    blk = lambda a: (None,) * (a.ndim - 2) + (h, tw)
    return pl.pallas_call(
        _inv1_kernel,
        grid=(b, w // tw),
        in_specs=[
            pl.BlockSpec((h, m2), lambda bb, t: (0, 0)),
            pl.BlockSpec((None, m2, tw), lambda bb, t: (bb, 0, t)),
            pl.BlockSpec(blk(gate), gate_index),
            pl.BlockSpec(blk(v), v_index),
            pl.BlockSpec((1, tw), lambda bb, t: (0, 0)),
        ],
        out_specs=pl.BlockSpec((None, h, tw), lambda bb, t: (bb, 0, t)),
        out_shape=jax.ShapeDtypeStruct((b, h, w), F32),
        compiler_params=_cparams("arbitrary", "arbitrary"),
        name="conv_inv1",
    )(f3_bf, bspec, gate, v, skip_tiled)


def _build_g(g_ref, f2r, f2i, twr, twi):
    gr = f2r * twr - f2i * twi
    gi = f2r * twi + f2i * twr
    g_ref[0:N2, 0:N2] = gr.astype(BF16)
    g_ref[0:N2, N2:] = (-gi).astype(BF16)
    g_ref[N2:, 0:N2] = gi.astype(BF16)
    g_ref[N2:, N2:] = gr.astype(BF16)


def _conv2_kernel(a_ref, f2r_ref, f2i_ref, twr_ref, twi_ref, k_ref, o_ref, g_ref, *, kc):
    f2r = f2r_ref[...]
    f2i = f2i_ref[...]

    def body(q, carry):
        _build_g(g_ref, f2r, f2i, twr_ref[pl.ds(q, 1), :], twi_ref[pl.ds(q, 1), :])
        g = g_ref[...]
        a = jnp.concatenate([a_ref[0, q], a_ref[1, q]], axis=0)
        x = jnp.dot(g, a, preferred_element_type=F32)
        xr, xi = x[:N2], x[N2:]
        kr, ki = k_ref[q, :N2, :], k_ref[q, N2:, :]
        y = jnp.concatenate([xr * kr - xi * ki, xr * ki + xi * kr], axis=0).astype(BF16)
        bk = lax.dot_general(g, y, (((0,), (0,)), ((), ())), preferred_element_type=F32)
        o_ref[0, q] = bk[:N2].astype(o_ref.dtype)
        o_ref[1, q] = bk[N2:].astype(o_ref.dtype)
        return carry

    lax.fori_loop(0, kc, body, 0)


def _conv2(a5, f2r, f2i, twr, twi, kspec, order, *, kc, ct):
    b, _, h, _, c = a5.shape
    kern = functools.partial(_conv2_kernel, kc=kc)
    return pl.pallas_call(
        kern,
        grid=(b, h // kc, c // ct),
        in_specs=[
            pl.BlockSpec((None, 2, kc, N2, ct), lambda bb, k, j: (bb, 0, k, 0, j)),
            pl.BlockSpec((N2, N2), lambda bb, k, j: (0, 0)),
            pl.BlockSpec((N2, N2), lambda bb, k, j: (0, 0)),
            pl.BlockSpec((kc, N2), lambda bb, k, j: (k, 0)),
            pl.BlockSpec((kc, N2), lambda bb, k, j: (k, 0)),
            pl.BlockSpec((None, kc, 2 * N2, ct), lambda bb, k, j: (order, k, 0, j)),
        ],
        out_specs=pl.BlockSpec((None, 2, kc, N2, ct), lambda bb, k, j: (bb, 0, k, 0, j)),
        out_shape=jax.ShapeDtypeStruct(a5.shape, BF16),
        scratch_shapes=[pltpu.VMEM((2 * N2, 2 * N2), BF16)],
        compiler_params=_cparams("arbitrary", "arbitrary", "arbitrary"),
        name="conv_stage2",
    )(a5, f2r, f2i, twr, twi, kspec)


def _filt2_kernel(af_ref, ab_ref, f2r_ref, f2i_ref, twr_ref, twi_ref, o_ref, g_ref, *, kc):
    f2r = f2r_ref[...]
    f2i = f2i_ref[...]

    def body(q, carry):
        _build_g(g_ref, f2r, f2i, twr_ref[pl.ds(q, 1), :], twi_ref[pl.ds(q, 1), :])
        g = g_ref[...]
        af = jnp.concatenate([af_ref[0, q], af_ref[1, q]], axis=0)
        ab = jnp.concatenate([ab_ref[0, q], ab_ref[1, q]], axis=0)
        xf = jnp.dot(g, af, preferred_element_type=F32)
        xb = jnp.dot(g, ab, preferred_element_type=F32)
        o_ref[q, :N2, :] = xf[:N2] + xb[:N2]
        o_ref[q, N2:, :] = xf[N2:] - xb[N2:]
        return carry

    lax.fori_loop(0, kc, body, 0)


def _filt2(af4, f2r, f2i, twr, twi, *, kc, ct):
    _, h, _, c4 = af4.shape
    nct = CH // ct
    kern = functools.partial(_filt2_kernel, kc=kc)
    return pl.pallas_call(
        kern,
        grid=(2, h // kc, nct),
        in_specs=[
            pl.BlockSpec((2, kc, N2, ct), lambda o, k, j: (0, k, 0, (2 * o) * nct + j)),
            pl.BlockSpec((2, kc, N2, ct), lambda o, k, j: (0, k, 0, (2 * o + 1) * nct + j)),
            pl.BlockSpec((N2, N2), lambda o, k, j: (0, 0)),
            pl.BlockSpec((N2, N2), lambda o, k, j: (0, 0)),
            pl.BlockSpec((kc, N2), lambda o, k, j: (k, 0)),
            pl.BlockSpec((kc, N2), lambda o, k, j: (k, 0)),
        ],
        out_specs=pl.BlockSpec((None, kc, 2 * N2, ct), lambda o, k, j: (o, k, 0, j)),
        out_shape=jax.ShapeDtypeStruct((2, h, 2 * N2, CH), F32),
        scratch_shapes=[pltpu.VMEM((2 * N2, 2 * N2), BF16)],
        compiler_params=_cparams("arbitrary", "arbitrary", "arbitrary"),
        name="filter_stage2",
    )(af4, af4, f2r, f2i, twr, twi)


def _chdft_kernel(u_ref, fc_ref, o_ref):
    gd = FNET_GROUP_DIM
    fc = fc_ref[...]
    for g in range(FNET_GROUPS):
        z = jnp.dot(u_ref[:, g * gd:(g + 1) * gd].astype(BF16), fc, preferred_element_type=F32)
        o_ref[0, :, g * gd:(g + 1) * gd] = z[:, :gd].astype(o_ref.dtype)
        o_ref[1, :, g * gd:(g + 1) * gd] = z[:, gd:].astype(o_ref.dtype)


def _chdft(p, fc_bf, *, tm):
    b, _, L, _ = p.shape
    gd = FNET_GROUP_DIM
    return pl.pallas_call(
        _chdft_kernel,
        grid=(b, L // tm),
        in_specs=[
            pl.BlockSpec((None, None, tm, CH), lambda bb, i: (bb, 4, i, 0)),
            pl.BlockSpec((gd, 2 * gd), lambda bb, i: (0, 0)),
        ],
        out_specs=pl.BlockSpec((None, 2, tm, CH), lambda bb, i: (bb, 0, i, 0)),
        out_shape=jax.ShapeDtypeStruct((b, 2, L, CH), BF16),
        compiler_params=_cparams("arbitrary", "arbitrary"),
        name="fnet_chdft",
    )(p, fc_bf)


def _fnet2_kernel(a_ref, f2r_ref, f2i_ref, twr_ref, twi_ref, w_ref, b_ref, o_ref, g_ref, *, kc):
    f2r = f2r_ref[...]
    f2i = f2i_ref[...]
    gd = FNET_GROUP_DIM
    for q in range(kc):
        twr = twr_ref[q:q + 1, :]
        twi = twi_ref[q:q + 1, :]
        g_ref[:, 0:N2] = (f2r * twr - f2i * twi).astype(BF16)
        g_ref[:, N2:] = (-(f2r * twi + f2i * twr)).astype(BF16)
        a = jnp.concatenate([a_ref[0, q], a_ref[1, q]], axis=0)
        fg = jnp.dot(g_ref[...], a, preferred_element_type=F32).astype(BF16)
        for g in range(FNET_GROUPS):
            yg = jnp.dot(fg[:, g * gd:(g + 1) * gd], w_ref[g], preferred_element_type=F32)
            o_ref[:, q * CH + g * gd:q * CH + (g + 1) * gd] = yg + b_ref[:, g * gd:(g + 1) * gd]


def _fnet2(a5, f2r, f2i, twr, twi, fnet_w_bf, fnet_b, *, kc):
    b, _, n1, _, c = a5.shape
    kern = functools.partial(_fnet2_kernel, kc=kc)
    gd = FNET_GROUP_DIM
    return pl.pallas_call(
        kern,
        grid=(b, n1 // kc),
        in_specs=[
            pl.BlockSpec((None, 2, kc, N2, c), lambda bb, k: (bb, 0, k, 0, 0)),
            pl.BlockSpec((N2, N2), lambda bb, k: (0, 0)),
            pl.BlockSpec((N2, N2), lambda bb, k: (0, 0)),
            pl.BlockSpec((kc, N2), lambda bb, k: (k, 0)),
            pl.BlockSpec((kc, N2), lambda bb, k: (k, 0)),
            pl.BlockSpec((FNET_GROUPS, gd, gd), lambda bb, k: (0, 0, 0)),
            pl.BlockSpec((1, c), lambda bb, k: (0, 0)),
        ],
        out_specs=pl.BlockSpec((None, N2, kc * c), lambda bb, k: (bb, 0, k)),
        out_shape=jax.ShapeDtypeStruct((b, N2, n1 * c), F32),
        scratch_shapes=[pltpu.VMEM((N2, 2 * N2), BF16)],
        compiler_params=_cparams("arbitrary", "arbitrary"),
        name="fnet_stage2",
    )(a5, f2r, f2i, twr, twi, fnet_w_bf, fnet_b.reshape(1, c))


def _outproj_kernel(zz_ref, zhy_ref, yg_ref, zfn_ref, x_ref, nhy_ref, nfn_ref, w_ref, fin_ref,
                    o_ref, cat_ref):
    def silu(z):
        return z / (1.0 + jnp.exp(-z))

    def nrm(a, g):
        ms = jnp.mean(a * a, axis=-1, keepdims=True)
        return a * lax.rsqrt(ms + EPS) * g

    c = zz_ref.shape[-1]
    cat_ref[:, 0:c] = nrm(zz_ref[...] * silu(zhy_ref[...]), nhy_ref[...]).astype(BF16)
    cat_ref[:, c:] = nrm(yg_ref[...] * silu(zfn_ref[...]), nfn_ref[...]).astype(BF16)
    y = x_ref[...] + jnp.dot(cat_ref[...], w_ref[...], preferred_element_type=F32)
    o_ref[...] = nrm(y, fin_ref[...])


def _outproj(zz, p, yg, x, norm_hy, norm_fn, w_out_bf, final_norm, *, tm):
    b, L, d = x.shape
    c = CH
    vecc = lambda: pl.BlockSpec((1, c), lambda bb, i: (0, 0))
    return pl.pallas_call(
        _outproj_kernel,
        grid=(b, L // tm),
        in_specs=[
            pl.BlockSpec((None, tm, c), lambda bb, i: (bb, i, 0)),
            pl.BlockSpec((None, None, tm, c), lambda bb, i: (bb, 3, i, 0)),
            pl.BlockSpec((None, tm, c), lambda bb, i: (bb, i, 0)),
            pl.BlockSpec((None, None, tm, c), lambda bb, i: (bb, 5, i, 0)),
            pl.BlockSpec((None, tm, d), lambda bb, i: (bb, i, 0)),
            vecc(), vecc(),
            pl.BlockSpec((2 * c, d), lambda bb, i: (0, 0)),
            pl.BlockSpec((1, d), lambda bb, i: (0, 0)),
        ],
        out_specs=pl.BlockSpec((None, tm, d), lambda bb, i: (bb, i, 0)),
        out_shape=jax.ShapeDtypeStruct((b, L, d), F32),
        scratch_shapes=[pltpu.VMEM((tm, 2 * c), BF16)],
        compiler_params=_cparams("arbitrary", "arbitrary"),
        name="outproj",
    )(zz, p, yg, p, x, norm_hy.reshape(1, c), norm_fn.reshape(1, c), w_out_bf, final_norm.reshape(1, d))


def _trunk(x, norm_g, w_in_bf, conv_w, conv_b, w1p, b1, w2, b2, w3, b3, w4, freq, skip,
           fnet_w_bf, fnet_b, norm_hy, norm_fn, w_out_bf, final_norm):
    b, L, d = x.shape
    n = 2 * L
    h = n // N2 // 2
    n1f = L // N2
    wcols = N2 * CH
    tw = 4096
    kc = 8

    f1, f3, twr, twi = (jnp.asarray(t) for t in _conv_tables(L))
    f2r, f2i = (jnp.asarray(t) for t in _f2_tables())
    fn1, ftwr, ftwi, fc = (jnp.asarray(t) for t in _fnet_tables(L))
    f1_bf, f3_bf, fn1_bf, fc_bf = (t.astype(BF16) for t in (f1, f3, fn1, fc))

    p = _inproj(x, norm_g, w_in_bf, conv_w, conv_b, tm=min(512, L))
    p2 = p.reshape(b, N_PROJ_GROUPS, h, wcols)

    filt = _filters(L, w1p, b1, w2, b2, w3, b3, freq, w4, tr=min(512, L))
    af = _lmm(f1_bf, filt.reshape(h, N2 * 4 * CH), lambda bb, t: (0, t), BF16, tw=tw, name="filter_fwd1")
    kspec = _filt2(af.reshape(2, h, N2, 4 * CH), f2r, f2i, twr, twi, kc=kc, ct=CH)

    skip_t = jnp.tile(skip, (1, tw // CH))

    a = _lmm(f1_bf, p2, lambda bb, t: (bb, 2, 0, t), BF16, tw=tw, name="conv_fwd1")
    bsp = _conv2(a.reshape(b, 2, h, N2, CH), f2r, f2i, twr, twi, kspec, 0, kc=kc, ct=CH)
    zz1 = _inv1(f3_bf, bsp.reshape(b, 2 * h, wcols), p2, lambda bb, t: (bb, 0, 0, t),
                p2, lambda bb, t: (bb, 2, 0, t), skip_t[0:1], tw=tw)
    a = _lmm(f1_bf, zz1, lambda bb, t: (bb, 0, t), BF16, tw=tw, name="conv_fwd1")
    bsp = _conv2(a.reshape(b, 2, h, N2, CH), f2r, f2i, twr, twi, kspec, 1, kc=kc, ct=CH)
    zz2 = _inv1(f3_bf, bsp.reshape(b, 2 * h, wcols), p2, lambda bb, t: (bb, 1, 0, t),
                zz1, lambda bb, t: (bb, 0, t), skip_t[1:2], tw=tw)

    z = _chdft(p, fc_bf, tm=min(512, L))
    az = _lmm(fn1_bf, z.reshape(b, 2 * n1f, wcols), lambda bb, t: (bb, 0, t), BF16, tw=tw, name="fnet_fwd1")
    yg = _fnet2(az.reshape(b, 2, n1f, N2, CH), f2r, f2i, ftwr, ftwi, fnet_w_bf, fnet_b, kc=kc)

    return _outproj(zz2.reshape(b, L, CH), p, yg.reshape(b, L, CH), x, norm_hy, norm_fn,
                    w_out_bf, final_norm, tm=min(512, L))


def kernel(x_prompt, x_sample, norm_g, w_in, conv_w, conv_b, filt_w1, filt_b1, filt_w2, filt_b2,
           filt_w3, filt_b3, filt_w4, filt_freq, filt_skip, fnet_w, fnet_b, norm_hy, norm_fn,
           w_out, final_norm):
    l = 0
    w1p = jnp.zeros((FILTER_HIDDEN, FILTER_HIDDEN), F32).at[:FILTER_EMB].set(filt_w1[l])
    args = (norm_g[l], w_in[l].astype(BF16), conv_w[l], conv_b[l], w1p, filt_b1[l], filt_w2[l],
            filt_b2[l], filt_w3[l], filt_b3[l], filt_w4[l], filt_freq[l], filt_skip[l],
            fnet_w[l].astype(BF16), fnet_b[l], norm_hy[l], norm_fn[l], w_out[l].astype(BF16),
            final_norm)
    return (_trunk(x_prompt, *args), _trunk(x_sample, *args))
```

```python
import functools
import math

import numpy as np
import jax
import jax.numpy as jnp
from jax import lax
from jax.experimental import pallas as pl
from jax.experimental.pallas import tpu as pltpu

F32 = jnp.float32
BF16 = jnp.bfloat16

D_MODEL = 2048
D_HYENA = D_MODEL // 2
D_FNET = D_MODEL - D_HYENA
HYENA_HEADS = 8
HYENA_HEAD_DIM = D_HYENA // HYENA_HEADS
FNET_GROUPS = 4
FNET_GROUP_DIM = D_FNET // FNET_GROUPS
FILTER_EMB = 33
FILTER_HIDDEN = 64
MIN_DECAY = math.log(1e-2) / 1.5
MAX_DECAY = math.log(1e-2) / 0.3
EPS = 1e-6

N_PROJ_GROUPS = 6
CH = 1024
N2 = 128
HALO = 16
VMEM_LIMIT = 52 * 1024 * 1024


def _cparams(*sem):
    return pltpu.CompilerParams(dimension_semantics=sem, vmem_limit_bytes=VMEM_LIMIT)


def _conv_tables(L):
    n = 2 * L
    n1 = n // N2
    h = n1 // 2
    k1 = np.arange(h, dtype=np.float64)
    ang1 = 2.0 * np.pi * np.outer(k1 + 0.5, np.arange(h)) / n1
    f1 = np.concatenate([np.cos(ang1), -np.sin(ang1)], axis=0)
    f3 = (2.0 / n) * np.stack([np.cos(ang1).T, -np.sin(ang1).T], axis=-1).reshape(h, 2 * h)
    angw = 2.0 * np.pi * np.outer(k1 + 0.5, np.arange(N2)) / n
    return (f1.astype(np.float32), f3.astype(np.float32),
            np.cos(angw).astype(np.float32), (-np.sin(angw)).astype(np.float32))


def _f2_tables():
    ang = 2.0 * np.pi * np.outer(np.arange(N2), np.arange(N2)) / N2
    return np.cos(ang).astype(np.float32), (-np.sin(ang)).astype(np.float32)


def _fnet_tables(L):
    n1 = L // N2
    ang1 = 2.0 * np.pi * np.outer(np.arange(n1), np.arange(n1)) / n1
    c, s = np.cos(ang1), np.sin(ang1)
    fn1 = np.block([[c, s], [-s, c]])
    angw = 2.0 * np.pi * np.outer(np.arange(n1), np.arange(N2)) / L
    scale = 1.0 / math.sqrt(L * FNET_GROUP_DIM)
    gd = FNET_GROUP_DIM
    angc = 2.0 * np.pi * np.outer(np.arange(gd), np.arange(gd)) / gd
    fc = np.concatenate([np.cos(angc), -np.sin(angc)], axis=1)
    return (fn1.astype(np.float32), (scale * np.cos(angw)).astype(np.float32),
            (-scale * np.sin(angw)).astype(np.float32), fc.astype(np.float32))


def _positional_features(L):
    bands = (FILTER_EMB - 1) // 2
    t = np.linspace(0.0, 1.0, L)
    w = 2.0 * np.pi * np.arange(L) / L
    f = np.linspace(1e-4, bands - 1, bands)
    fw = w[:, None] * f[None, :]
    z = np.concatenate([t[:, None], np.cos(fw), -np.sin(fw)], axis=-1)
    zp = np.zeros((L, FILTER_HIDDEN), np.float64)
    zp[:, :FILTER_EMB] = z
    return zp.astype(np.float32)


def _inproj_kernel(xm_ref, xt_ref, xb_ref, g_ref, w_ref, cw_ref, cb_ref, o_ref, hn_ref, *, tm, n_conv):
    i = pl.program_id(1)
    j = pl.program_id(2)
    last_i = pl.num_programs(1) - 1

    @pl.when(j == 0)
    def _():
        g = g_ref[...]

        def nrm(x):
            ms = jnp.mean(x * x, axis=-1, keepdims=True)
            return x * lax.rsqrt(ms + EPS) * g

        hn_ref[HALO:HALO + tm, :] = nrm(xm_ref[...]).astype(BF16)
        top = jnp.where(i > 0, nrm(xt_ref[...]), 0.0)
        bot = jnp.where(i < last_i, nrm(xb_ref[...]), 0.0)
        hn_ref[0:HALO, :] = top.astype(BF16)
        hn_ref[HALO + tm:, :] = bot.astype(BF16)

    @pl.when(j < n_conv)
    def _():
        rows = tm + 2 * HALO
        p = jnp.dot(hn_ref[...], w_ref[...], preferred_element_type=F32)
        prev = pltpu.roll(p, 1, axis=0)
        nxt = pltpu.roll(p, rows - 1, axis=0)
        cw = cw_ref[...]
        u = prev * cw[0:1, :] + p * cw[1:2, :] + nxt * cw[2:3, :] + cb_ref[...]
        o_ref[...] = u[HALO:HALO + tm, :]

    @pl.when(j >= n_conv)
    def _():
        o_ref[...] = jnp.dot(hn_ref[HALO:HALO + tm, :], w_ref[...], preferred_element_type=F32)


def _inproj(x, norm_g, w_in_bf, conv_w, conv_b, *, tm):
    b, L, d = x.shape
    n_conv = 3
    rt = tm // HALO
    nhalo = L // HALO
    kern = functools.partial(_inproj_kernel, tm=tm, n_conv=n_conv)
    return pl.pallas_call(
        kern,
        grid=(b, L // tm, N_PROJ_GROUPS),
        in_specs=[
            pl.BlockSpec((None, tm, d), lambda bb, i, j: (bb, i, 0)),
            pl.BlockSpec((None, HALO, d), lambda bb, i, j: (bb, jnp.maximum(i * rt - 1, 0), 0)),
            pl.BlockSpec((None, HALO, d), lambda bb, i, j: (bb, jnp.minimum((i + 1) * rt, nhalo - 1), 0)),
            pl.BlockSpec((1, d), lambda bb, i, j: (0, 0)),
            pl.BlockSpec((d, CH), lambda bb, i, j: (0, j)),
            pl.BlockSpec((3, CH), lambda bb, i, j: (0, jnp.minimum(j, n_conv - 1))),
            pl.BlockSpec((1, CH), lambda bb, i, j: (0, jnp.minimum(j, n_conv - 1))),
        ],
        out_specs=pl.BlockSpec((None, None, tm, CH), lambda bb, i, j: (bb, j, i, 0)),
        out_shape=jax.ShapeDtypeStruct((b, N_PROJ_GROUPS, L, CH), F32),
        scratch_shapes=[pltpu.VMEM((tm + 2 * HALO, d), BF16)],
        compiler_params=_cparams("arbitrary", "arbitrary", "arbitrary"),
        name="inproj",
    )(x, x, x, norm_g.reshape(1, d), w_in_bf, conv_w, conv_b.reshape(1, -1))


def _filter_kernel(z_ref, w1_ref, b1_ref, w2_ref, b2_ref, w3_ref, b3_ref, fr_ref, w4_ref, dl_ref,
                   o_ref, h_ref, *, tr, L):
    i = pl.program_id(0)
    j = pl.program_id(1)
    hp = lax.Precision.HIGHEST

    @pl.when(j == 0)
    def _():
        fr = fr_ref[...]
        h = jnp.sin(fr * (jnp.dot(z_ref[...], w1_ref[...], precision=hp, preferred_element_type=F32) + b1_ref[...]))
        h = jnp.sin(fr * (jnp.dot(h, w2_ref[...], precision=hp, preferred_element_type=F32) + b2_ref[...]))
        h = jnp.sin(fr * (jnp.dot(h, w3_ref[...], precision=hp, preferred_element_type=F32) + b3_ref[...]))
        h_ref[...] = h

    f = jnp.dot(h_ref[...], w4_ref[...], precision=hp, preferred_element_type=F32)
    row = i * tr + lax.broadcasted_iota(jnp.int32, (tr, 1), 0)
    t = row.astype(F32) * (1.0 / (L - 1))
    f = f * jnp.exp(-t * jnp.abs(dl_ref[...]))
    drop = jnp.logical_and(row == 0, (j % 2) == 1)
    o_ref[...] = jnp.where(drop, 0.0, f)


def _filters(L, w1p, b1, w2, b2, w3, b3, freq, w4, *, tr):
    z = jnp.asarray(_positional_features(L))
    deltas = np.tile(np.linspace(MIN_DECAY, MAX_DECAY, HYENA_HEAD_DIM), HYENA_HEADS).astype(np.float32)
    fh = FILTER_HIDDEN
    ncol = w4.shape[1] // CH
    vec = lambda: pl.BlockSpec((1, fh), lambda i, j: (0, 0))
    mat = lambda: pl.BlockSpec((fh, fh), lambda i, j: (0, 0))
    kern = functools.partial(_filter_kernel, tr=tr, L=L)
    return pl.pallas_call(
        kern,
        grid=(L // tr, ncol),
        in_specs=[
            pl.BlockSpec((tr, fh), lambda i, j: (i, 0)),
            mat(), vec(), mat(), vec(), mat(), vec(), vec(),
            pl.BlockSpec((fh, CH), lambda i, j: (0, j)),
            pl.BlockSpec((1, CH), lambda i, j: (0, 0)),
        ],
        out_specs=pl.BlockSpec((tr, CH), lambda i, j: (i, j)),
        out_shape=jax.ShapeDtypeStruct((L, ncol * CH), F32),
        scratch_shapes=[pltpu.VMEM((tr, fh), F32)],
        compiler_params=_cparams("arbitrary", "arbitrary"),
        name="filter_mlp",
    )(z, w1p, b1.reshape(1, fh), w2, b2.reshape(1, fh), w3, b3.reshape(1, fh), freq.reshape(1, fh),
      w4, jnp.asarray(deltas).reshape(1, CH))


LANES = 128


def _s1_kernel(m_ref, x_ref, o_ref, *, rows, s, strided):
    m = m_ref[...]
    x2 = x_ref.reshape(rows * s, LANES) if strided else None

    def body(j, carry):
        xj = x2[pl.ds(j, rows, stride=s), :] if strided else x_ref[j]
        o_ref[j] = jnp.dot(m, xj.astype(BF16), preferred_element_type=F32)
        return carry

    lax.fori_loop(0, s, body, 0)


def _s1(mat_bf, x, x_block, x_index, *, rows, s, strided, name):
    m, k = mat_bf.shape
    assert k == rows
    b = x.shape[0]
    c = x.shape[-1]
    kern = functools.partial(_s1_kernel, rows=rows, s=s, strided=strided)
    return pl.pallas_call(
        kern,
        grid=(b, N2 // s, c // LANES),
        in_specs=[
            pl.BlockSpec((m, k), lambda bb, t, j: (0, 0)),
            pl.BlockSpec(x_block, x_index),
        ],
        out_specs=pl.BlockSpec((None, s, m, LANES), lambda bb, t, j: (bb, t, 0, j)),
        out_shape=jax.ShapeDtypeStruct((b, N2, m, c), F32),
        compiler_params=_cparams("arbitrary", "arbitrary", "arbitrary"),
        name=name,
    )(mat_bf, x)


def _g_matrix(f2r, f2i, twr, twi):
    gr = f2r * twr - f2i * twi
    gi = f2r * twi + f2i * twr
    top = jnp.concatenate([gr, -gi], axis=1)
    bot = jnp.concatenate([gi, gr], axis=1)
    return jnp.concatenate([top, bot], axis=0).astype(BF16)


def _load_a(re_ref, im_ref, q, kc):
    re2 = re_ref.reshape(N2 * kc, LANES)
    im2 = im_ref.reshape(N2 * kc, LANES)
    ar = re2[pl.ds(q, N2, stride=kc), :]
    ai = im2[pl.ds(q, N2, stride=kc), :]
    return jnp.concatenate([ar, ai], axis=0).astype(BF16)


def _conv2_kernel(re_ref, im_ref, f2r_ref, f2i_ref, twr_ref, twi_ref, k_ref, o_ref, *, kc):
    f2r = f2r_ref[...]
    f2i = f2i_ref[...]
    for q in range(kc):
        g = _g_matrix(f2r, f2i, twr_ref[q:q + 1, :], twi_ref[q:q + 1, :])
        x = jnp.dot(g, _load_a(re_ref, im_ref, q, kc), preferred_element_type=F32)
        xr, xi = x[:N2], x[N2:]
        kr, ki = k_ref[q, :N2, :], k_ref[q, N2:, :]
        y = jnp.concatenate([xr * kr - xi * ki, xr * ki + xi * kr], axis=0).astype(BF16)
        o_ref[q] = lax.dot_general(g, y, (((0,), (0,)), ((), ())), preferred_element_type=F32)


def _conv2(a1, f2r, f2i, twr, twi, kspec, order, *, kc):
    b, _, m, c = a1.shape
    h = m // 2
    hb = h // kc
    kern = functools.partial(_conv2_kernel, kc=kc)
    return pl.pallas_call(
        kern,
        grid=(b, hb, c // LANES),
        in_specs=[
            pl.BlockSpec((None, N2, kc, LANES), lambda bb, k, j: (bb, 0, k, j)),
            pl.BlockSpec((None, N2, kc, LANES), lambda bb, k, j: (bb, 0, hb + k, j)),
            pl.BlockSpec((N2, N2), lambda bb, k, j: (0, 0)),
            pl.BlockSpec((N2, N2), lambda bb, k, j: (0, 0)),
            pl.BlockSpec((kc, N2), lambda bb, k, j: (k, 0)),
            pl.BlockSpec((kc, N2), lambda bb, k, j: (k, 0)),
            pl.BlockSpec((None, kc, 2 * N2, LANES), lambda bb, k, j: (order, k, 0, j)),
        ],
        out_specs=pl.BlockSpec((None, kc, 2 * N2, LANES), lambda bb, k, j: (bb, k, 0, j)),
        out_shape=jax.ShapeDtypeStruct((b, h, 2 * N2, c), F32),
        compiler_params=_cparams("arbitrary", "arbitrary", "arbitrary"),
        name="conv_stage2",
    )(a1, a1, f2r, f2i, twr, twi, kspec)


def _filt2_kernel(ref_ref, imf_ref, reb_ref, imb_ref, f2r_ref, f2i_ref, twr_ref, twi_ref, o_ref, *, kc):
    f2r = f2r_ref[...]
    f2i = f2i_ref[...]
    for q in range(kc):
        g = _g_matrix(f2r, f2i, twr_ref[q:q + 1, :], twi_ref[q:q + 1, :])
        xf = jnp.dot(g, _load_a(ref_ref, imf_ref, q, kc), preferred_element_type=F32)
        xb = jnp.dot(g, _load_a(reb_ref, imb_ref, q, kc), preferred_element_type=F32)
        o_ref[q, :N2, :] = xf[:N2] + xb[:N2]
        o_ref[q, N2:, :] = xf[N2:] - xb[N2:]


def _filt2(a1f, f2r, f2i, twr, twi, *, kc):
    _, _, m, _ = a1f.shape
    h = m // 2
    hb = h // kc
    nct = CH // LANES
    kern = functools.partial(_filt2_kernel, kc=kc)
    blk = lambda im, d: pl.BlockSpec(
        (None, N2, kc, LANES), lambda o, k, j: (0, 0, im * hb + k, (2 * o + d) * nct + j))
    return pl.pallas_call(
        kern,
        grid=(2, hb, nct),
        in_specs=[
            blk(0, 0), blk(1, 0), blk(0, 1), blk(1, 1),
            pl.BlockSpec((N2, N2), lambda o, k, j: (0, 0)),
            pl.BlockSpec((N2, N2), lambda o, k, j: (0, 0)),
            pl.BlockSpec((kc, N2), lambda o, k, j: (k, 0)),
            pl.BlockSpec((kc, N2), lambda o, k, j: (k, 0)),
        ],
        out_specs=pl.BlockSpec((None, kc, 2 * N2, LANES), lambda o, k, j: (o, k, 0, j)),
        out_shape=jax.ShapeDtypeStruct((2, h, 2 * N2, CH), F32),
        compiler_params=_cparams("arbitrary", "arbitrary", "arbitrary"),
        name="filter_stage2",
    )(a1f, a1f, a1f, a1f, f2r, f2i, twr, twi)


def _s3_kernel(m_ref, b_ref, gate_ref, v_ref, skip_ref, o_ref, *, h, s, v_strided, out_strided):
    m = m_ref[...]
    skip = skip_ref[...]
    b2 = b_ref.reshape(h * 2 * s, LANES)
    gate2 = gate_ref.reshape(h * s, LANES)
    v2 = v_ref.reshape(h * s, LANES) if v_strided else None

    def body(j, carry):
        bj = b2[pl.ds(j, 2 * h, stride=s), :].astype(BF16)
        y = jnp.dot(m, bj, preferred_element_type=F32)
        vj = v2[pl.ds(j, h, stride=s), :] if v_strided else v_ref[j]
        zz = gate2[pl.ds(j, h, stride=s), :] * (y + skip * vj)
        if out_strided:
            o_ref[:, pl.ds(j, 1), :] = zz.reshape(h, 1, LANES)
        else:
            o_ref[j] = zz
        return carry

    lax.fori_loop(0, s, body, 0)


def _s3(f3i_bf, bspec, gate5, gate_group, v, v_group, skip3, order, *, s, out_natural):
    b, h, _, c = bspec.shape
    b5 = bspec.reshape(b, h, 2, N2, c)
    nat = (None, None, h, s, LANES)
    v_strided = v_group is not None
    if v_strided:
        v_spec = pl.BlockSpec(nat, lambda bb, t, j: (bb, v_group, 0, t, j))
    else:
        v_spec = pl.BlockSpec((None, s, h, LANES), lambda bb, t, j: (bb, t, 0, j))
    if out_natural:
        out_spec = pl.BlockSpec((None, h, s, LANES), lambda bb, t, j: (bb, 0, t, j))
        out_shape = jax.ShapeDtypeStruct((b, h, N2, c), F32)
    else:
        out_spec = pl.BlockSpec((None, s, h, LANES), lambda bb, t, j: (bb, t, 0, j))
        out_shape = jax.ShapeDtypeStruct((b, N2, h, c), F32)
    kern = functools.partial(_s3_kernel, h=h, s=s, v_strided=v_strided, out_strided=out_natural)
    return pl.pallas_call(
        kern,
        grid=(b, N2 // s, c // LANES),
        in_specs=[
            pl.BlockSpec((h, 2 * h), lambda bb, t, j: (0, 0)),
            pl.BlockSpec((None, h, 2, s, LANES), lambda bb, t, j: (bb, 0, 0, t, j)),
            pl.BlockSpec(nat, lambda bb, t, j: (bb, gate_group, 0, t, j)),
            v_spec,
            pl.BlockSpec((None, 1, LANES), lambda bb, t, j: (order, 0, j)),
        ],
        out_specs=out_spec,
        out_shape=out_shape,
        compiler_params=_cparams("arbitrary", "arbitrary", "arbitrary"),
        name="conv_stage3",
    )(f3i_bf, b5, gate5, v, skip3)


def _chdft_kernel(u_ref, fc_ref, w_ref, o_ref):
    gd = FNET_GROUP_DIM
    fc = fc_ref[...]
    for g in range(FNET_GROUPS):
        z = jnp.dot(u_ref[:, g * gd:(g + 1) * gd].astype(BF16), fc, preferred_element_type=F32)
        w = w_ref[g]
        o_ref[0, :, g * gd:(g + 1) * gd] = jnp.dot(z[:, :gd].astype(BF16), w, preferred_element_type=F32)
        o_ref[1, :, g * gd:(g + 1) * gd] = jnp.dot(z[:, gd:].astype(BF16), w, preferred_element_type=F32)


def _chdft(p, fc_bf, fnet_w_bf, *, tm):
    b, _, L, _ = p.shape
    gd = FNET_GROUP_DIM
    return pl.pallas_call(
        _chdft_kernel,
        grid=(b, L // tm),
        in_specs=[
            pl.BlockSpec((None, None, tm, CH), lambda bb, i: (bb, 4, i, 0)),
            pl.BlockSpec((gd, 2 * gd), lambda bb, i: (0, 0)),
            pl.BlockSpec((FNET_GROUPS, gd, gd), lambda bb, i: (0, 0, 0)),
        ],
        out_specs=pl.BlockSpec((None, 2, tm, CH), lambda bb, i: (bb, 0, i, 0)),
        out_shape=jax.ShapeDtypeStruct((b, 2, L, CH), F32),
        compiler_params=_cparams("arbitrary", "arbitrary"),
        name="fnet_chdft",
    )(p, fc_bf, fnet_w_bf)


def _fnet2_kernel(re_ref, im_ref, f2r_ref, f2i_ref, twr_ref, twi_ref, b_ref, o_ref, *, kc):
    f2r = f2r_ref[...]
    f2i = f2i_ref[...]
    for q in range(kc):
        twr = twr_ref[q:q + 1, :]
        twi = twi_ref[q:q + 1, :]
        g = jnp.concatenate([f2r * twr - f2i * twi, -(f2r * twi + f2i * twr)], axis=1).astype(BF16)
        fg = jnp.dot(g, _load_a(re_ref, im_ref, q, kc), preferred_element_type=F32)
        o_ref[:, q:q + 1, :] = (fg + b_ref[...]).reshape(N2, 1, LANES)


def _fnet2(a1, f2r, f2i, twr, twi, fnet_b, *, kc):
    b, _, m, c = a1.shape
    n1 = m // 2
    nb = n1 // kc
    kern = functools.partial(_fnet2_kernel, kc=kc)
    return pl.pallas_call(
        kern,
        grid=(b, nb, c // LANES),
        in_specs=[
            pl.BlockSpec((None, N2, kc, LANES), lambda bb, k, j: (bb, 0, k, j)),
            pl.BlockSpec((None, N2, kc, LANES), lambda bb, k, j: (bb, 0, nb + k, j)),
            pl.BlockSpec((N2, N2), lambda bb, k, j: (0, 0)),
            pl.BlockSpec((N2, N2), lambda bb, k, j: (0, 0)),
            pl.BlockSpec((kc, N2), lambda bb, k, j: (k, 0)),
            pl.BlockSpec((kc, N2), lambda bb, k, j: (k, 0)),
            pl.BlockSpec((1, LANES), lambda bb, k, j: (0, j)),
        ],
        out_specs=pl.BlockSpec((None, N2, kc, LANES), lambda bb, k, j: (bb, 0, k, j)),
        out_shape=jax.ShapeDtypeStruct((b, N2, n1, c), F32),
        compiler_params=_cparams("arbitrary", "arbitrary", "arbitrary"),
        name="fnet_stage2",
    )(a1, a1, f2r, f2i, twr, twi, fnet_b.reshape(1, c))


def _outproj_kernel(zz_ref, zhy_ref, yg_ref, zfn_ref, x_ref, nhy_ref, nfn_ref, w_ref, fin_ref,
                    o_ref, cat_ref):
    def silu(z):
        return z / (1.0 + jnp.exp(-z))

    def nrm(a, g):
        ms = jnp.mean(a * a, axis=-1, keepdims=True)
        return a * lax.rsqrt(ms + EPS) * g

    c = zz_ref.shape[-1]
    cat_ref[:, 0:c] = nrm(zz_ref[...] * silu(zhy_ref[...]), nhy_ref[...]).astype(BF16)
    cat_ref[:, c:] = nrm(yg_ref[...] * silu(zfn_ref[...]), nfn_ref[...]).astype(BF16)
    y = x_ref[...] + jnp.dot(cat_ref[...], w_ref[...], preferred_element_type=F32)
    o_ref[...] = nrm(y, fin_ref[...])


def _outproj(zz, p, yg, x, norm_hy, norm_fn, w_out_bf, final_norm, *, tm):
    b, L, d = x.shape
    c = CH
    vecc = lambda: pl.BlockSpec((1, c), lambda bb, i: (0, 0))
    return pl.pallas_call(
        _outproj_kernel,
        grid=(b, L // tm),
        in_specs=[
            pl.BlockSpec((None, tm, c), lambda bb, i: (bb, i, 0)),
            pl.BlockSpec((None, None, tm, c), lambda bb, i: (bb, 3, i, 0)),
            pl.BlockSpec((None, tm, c), lambda bb, i: (bb, i, 0)),
            pl.BlockSpec((None, None, tm, c), lambda bb, i: (bb, 5, i, 0)),
            pl.BlockSpec((None, tm, d), lambda bb, i: (bb, i, 0)),
            vecc(), vecc(),
            pl.BlockSpec((2 * c, d), lambda bb, i: (0, 0)),
            pl.BlockSpec((1, d), lambda bb, i: (0, 0)),
        ],
        out_specs=pl.BlockSpec((None, tm, d), lambda bb, i: (bb, i, 0)),
        out_shape=jax.ShapeDtypeStruct((b, L, d), F32),
        scratch_shapes=[pltpu.VMEM((tm, 2 * c), BF16)],
        compiler_params=_cparams("arbitrary", "arbitrary"),
        name="outproj",
    )(zz, p, yg, p, x, norm_hy.reshape(1, c), norm_fn.reshape(1, c), w_out_bf, final_norm.reshape(1, d))


def _trunk(x, norm_g, w_in_bf, conv_w, conv_b, w1p, b1, w2, b2, w3, b3, w4, freq, skip,
           fnet_w_bf, fnet_b, norm_hy, norm_fn, w_out_bf, final_norm):
    b, L, d = x.shape
    n = 2 * L
    h = n // N2 // 2
    n1f = L // N2
    s = min(N2, 4096 // h)
    kc = 8

    f1, f3, twr, twi = (jnp.asarray(t) for t in _conv_tables(L))
    f2r, f2i = (jnp.asarray(t) for t in _f2_tables())
    fn1, ftwr, ftwi, fc = (jnp.asarray(t) for t in _fnet_tables(L))
    f1_bf, f3i_bf, fn1_bf, fc_bf = (t.astype(BF16) for t in (f1, f3, fn1, fc))

    p = _inproj(x, norm_g, w_in_bf, conv_w, conv_b, tm=min(512, L))
    p5 = p.reshape(b, N_PROJ_GROUPS, h, N2, CH)
    nat = (None, None, h, s, LANES)

    filt = _filters(L, w1p, b1, w2, b2, w3, b3, freq, w4, tr=min(512, L))
    a1f = _s1(f1_bf, filt.reshape(1, 1, h, N2, 4 * CH), nat, lambda bb, t, j: (0, 0, 0, t, j),
              rows=h, s=s, strided=True, name="filter_stage1")
    kspec = _filt2(a1f, f2r, f2i, twr, twi, kc=kc)

    skip3 = skip.reshape(2, 1, CH)

    a1 = _s1(f1_bf, p5, nat, lambda bb, t, j: (bb, 2, 0, t, j), rows=h, s=s, strided=True, name="conv_stage1")
    bsp = _conv2(a1, f2r, f2i, twr, twi, kspec, 0, kc=kc)
    zz1 = _s3(f3i_bf, bsp, p5, 0, p5, 2, skip3, 0, s=s, out_natural=False)
    a1 = _s1(f1_bf, zz1, (None, s, h, LANES), lambda bb, t, j: (bb, t, 0, j), rows=h, s=s, strided=False,
             name="conv_stage1")
    bsp = _conv2(a1, f2r, f2i, twr, twi, kspec, 1, kc=kc)
    zz2 = _s3(f3i_bf, bsp, p5, 1, zz1, None, skip3, 1, s=s, out_natural=True)

    z = _chdft(p, fc_bf, fnet_w_bf, tm=min(512, L))
    sf = min(N2, 4096 // (2 * n1f))
    a1z = _s1(fn1_bf, z.reshape(b, 2, n1f, N2, CH), (None, 2, n1f, sf, LANES),
              lambda bb, t, j: (bb, 0, 0, t, j), rows=2 * n1f, s=sf, strided=True, name="fnet_stage1")
    yg = _fnet2(a1z, f2r, f2i, ftwr, ftwi, fnet_b, kc=kc)

    return _outproj(zz2.reshape(b, L, CH), p, yg.reshape(b, L, CH), x, norm_hy, norm_fn,
                    w_out_bf, final_norm, tm=min(512, L))


def kernel(x_prompt, x_sample, norm_g, w_in, conv_w, conv_b, filt_w1, filt_b1, filt_w2, filt_b2,
           filt_w3, filt_b3, filt_w4, filt_freq, filt_skip, fnet_w, fnet_b, norm_hy, norm_fn,
           w_out, final_norm):
    l = 0
    w1p = jnp.zeros((FILTER_HIDDEN, FILTER_HIDDEN), F32).at[:FILTER_EMB].set(filt_w1[l])
    args = (norm_g[l], w_in[l].astype(BF16), conv_w[l], conv_b[l], w1p, filt_b1[l], filt_w2[l],
            filt_b2[l], filt_w3[l], filt_b3[l], filt_w4[l], filt_freq[l], filt_skip[l],
            fnet_w[l].astype(BF16), fnet_b[l], norm_hy[l], norm_fn[l], w_out[l].astype(BF16),
            final_norm)
    return (_trunk(x_prompt, *args), _trunk(x_sample, *args))
```

```python
import functools
import math

import numpy as np
import jax
import jax.numpy as jnp
from jax import lax
from jax.experimental import pallas as pl
from jax.experimental.pallas import tpu as pltpu

F32 = jnp.float32
BF16 = jnp.bfloat16

D_MODEL = 2048
D_HYENA = D_MODEL // 2
D_FNET = D_MODEL - D_HYENA
HYENA_HEADS = 8
HYENA_HEAD_DIM = D_HYENA // HYENA_HEADS
FNET_GROUPS = 4
FNET_GROUP_DIM = D_FNET // FNET_GROUPS
FILTER_EMB = 33
FILTER_HIDDEN = 64
MIN_DECAY = math.log(1e-2) / 1.5
MAX_DECAY = math.log(1e-2) / 0.3
EPS = 1e-6

LANES = 128
N_PROJ_GROUPS = 6
CH = 1024
NT = CH // LANES
N2 = 128
HALO = 16
VMEM_LIMIT = 52 * 1024 * 1024


def _cparams(*sem):
    return pltpu.CompilerParams(dimension_semantics=sem, vmem_limit_bytes=VMEM_LIMIT)


def _conv_tables(L):
    n = 2 * L
    n1 = n // N2
    h = n1 // 2
    k1 = np.arange(h, dtype=np.float64)
    ang1 = 2.0 * np.pi * np.outer(k1 + 0.5, np.arange(h)) / n1
    f1 = np.concatenate([np.cos(ang1), -np.sin(ang1)], axis=0)
    f3 = (2.0 / n) * np.stack([np.cos(ang1).T, -np.sin(ang1).T], axis=-1).reshape(h, 2 * h)
    angw = 2.0 * np.pi * np.outer(k1 + 0.5, np.arange(N2)) / n
    return (f1.astype(np.float32), f3.astype(np.float32),
            np.cos(angw).astype(np.float32), (-np.sin(angw)).astype(np.float32))


def _f2_tables():
    ang = 2.0 * np.pi * np.outer(np.arange(N2), np.arange(N2)) / N2
    return np.cos(ang).astype(np.float32), (-np.sin(ang)).astype(np.float32)


def _fnet_tables(L):
    n1 = L // N2
    ang1 = 2.0 * np.pi * np.outer(np.arange(n1), np.arange(n1)) / n1
    c, s = np.cos(ang1), np.sin(ang1)
    fn1 = np.block([[c, s], [-s, c]])
    angw = 2.0 * np.pi * np.outer(np.arange(n1), np.arange(N2)) / L
    scale = 1.0 / math.sqrt(L * FNET_GROUP_DIM)
    gd = FNET_GROUP_DIM
    angc = 2.0 * np.pi * np.outer(np.arange(gd), np.arange(gd)) / gd
    fc = np.concatenate([np.cos(angc), -np.sin(angc)], axis=1)
    return (fn1.astype(np.float32), (scale * np.cos(angw)).astype(np.float32),
            (-scale * np.sin(angw)).astype(np.float32), fc.astype(np.float32))


def _positional_features(L):
    bands = (FILTER_EMB - 1) // 2
    t = np.linspace(0.0, 1.0, L)
    w = 2.0 * np.pi * np.arange(L) / L
    f = np.linspace(1e-4, bands - 1, bands)
    fw = w[:, None] * f[None, :]
    z = np.concatenate([t[:, None], np.cos(fw), -np.sin(fw)], axis=-1)
    zp = np.zeros((L, FILTER_HIDDEN), np.float64)
    zp[:, :FILTER_EMB] = z
    return zp.astype(np.float32)


def _store_tiles(o_ref, val):
    for t in range(val.shape[-1] // LANES):
        o_ref[t] = val[:, t * LANES:(t + 1) * LANES]


def _inproj_kernel(xm_ref, xt_ref, xb_ref, g_ref, w_ref, cw_ref, cb_ref, o_ref, hn_ref, *, tm, n_conv):
    i = pl.program_id(1)
    j = pl.program_id(2)
    last_i = pl.num_programs(1) - 1

    @pl.when(j == 0)
    def _():
        g = g_ref[...]

        def nrm(x):
            ms = jnp.mean(x * x, axis=-1, keepdims=True)
            return x * lax.rsqrt(ms + EPS) * g

        hn_ref[HALO:HALO + tm, :] = nrm(xm_ref[...]).astype(BF16)
        top = jnp.where(i > 0, nrm(xt_ref[...]), 0.0)
        bot = jnp.where(i < last_i, nrm(xb_ref[...]), 0.0)
        hn_ref[0:HALO, :] = top.astype(BF16)
        hn_ref[HALO + tm:, :] = bot.astype(BF16)

    @pl.when(j < n_conv)
    def _():
        rows = tm + 2 * HALO
        p = jnp.dot(hn_ref[...], w_ref[...], preferred_element_type=F32)
        prev = pltpu.roll(p, 1, axis=0)
        nxt = pltpu.roll(p, rows - 1, axis=0)
        cw = cw_ref[...]
        u = prev * cw[0:1, :] + p * cw[1:2, :] + nxt * cw[2:3, :] + cb_ref[...]
        _store_tiles(o_ref, u[HALO:HALO + tm, :])

    @pl.when(j >= n_conv)
    def _():
        _store_tiles(o_ref, jnp.dot(hn_ref[HALO:HALO + tm, :], w_ref[...], preferred_element_type=F32))


def _inproj(x, norm_g, w_in_bf, conv_w, conv_b, *, tm):
    b, L, d = x.shape
    n_conv = 3
    rt = tm // HALO
    nhalo = L // HALO
    kern = functools.partial(_inproj_kernel, tm=tm, n_conv=n_conv)
    return pl.pallas_call(
        kern,
        grid=(b, L // tm, N_PROJ_GROUPS),
        in_specs=[
            pl.BlockSpec((None, tm, d), lambda bb, i, j: (bb, i, 0)),
            pl.BlockSpec((None, HALO, d), lambda bb, i, j: (bb, jnp.maximum(i * rt - 1, 0), 0)),
            pl.BlockSpec((None, HALO, d), lambda bb, i, j: (bb, jnp.minimum((i + 1) * rt, nhalo - 1), 0)),
            pl.BlockSpec((1, d), lambda bb, i, j: (0, 0)),
            pl.BlockSpec((d, CH), lambda bb, i, j: (0, j)),
            pl.BlockSpec((3, CH), lambda bb, i, j: (0, jnp.minimum(j, n_conv - 1))),
            pl.BlockSpec((1, CH), lambda bb, i, j: (0, jnp.minimum(j, n_conv - 1))),
        ],
        out_specs=pl.BlockSpec((None, None, NT, tm, LANES), lambda bb, i, j: (bb, j, 0, i, 0)),
        out_shape=jax.ShapeDtypeStruct((b, N_PROJ_GROUPS, NT, L, LANES), F32),
        scratch_shapes=[pltpu.VMEM((tm + 2 * HALO, d), BF16)],
        compiler_params=_cparams("arbitrary", "arbitrary", "arbitrary"),
        name="inproj",
    )(x, x, x, norm_g.reshape(1, d), w_in_bf, conv_w, conv_b.reshape(1, -1))


def _filter_kernel(z_ref, w1_ref, b1_ref, w2_ref, b2_ref, w3_ref, b3_ref, fr_ref, w4_ref, dl_ref,
                   o_ref, h_ref, *, tr, L):
    i = pl.program_id(0)
    j = pl.program_id(1)
    hp = lax.Precision.HIGHEST

    @pl.when(j == 0)
    def _():
        fr = fr_ref[...]
        h = jnp.sin(fr * (jnp.dot(z_ref[...], w1_ref[...], precision=hp, preferred_element_type=F32) + b1_ref[...]))
        h = jnp.sin(fr * (jnp.dot(h, w2_ref[...], precision=hp, preferred_element_type=F32) + b2_ref[...]))
        h = jnp.sin(fr * (jnp.dot(h, w3_ref[...], precision=hp, preferred_element_type=F32) + b3_ref[...]))
        h_ref[...] = h

    f = jnp.dot(h_ref[...], w4_ref[...], precision=hp, preferred_element_type=F32)
    row = i * tr + lax.broadcasted_iota(jnp.int32, (tr, 1), 0)
    t = row.astype(F32) * (1.0 / (L - 1))
    f = f * jnp.exp(-t * jnp.abs(dl_ref[...]))
    drop = jnp.logical_and(row == 0, (j % 2) == 1)
    _store_tiles(o_ref, jnp.where(drop, 0.0, f))


def _filters(L, w1p, b1, w2, b2, w3, b3, freq, w4, *, tr):
    z = jnp.asarray(_positional_features(L))
    deltas = np.tile(np.linspace(MIN_DECAY, MAX_DECAY, HYENA_HEAD_DIM), HYENA_HEADS).astype(np.float32)
    fh = FILTER_HIDDEN
    ncol = w4.shape[1] // CH
    vec = lambda: pl.BlockSpec((1, fh), lambda i, j: (0, 0))
    mat = lambda: pl.BlockSpec((fh, fh), lambda i, j: (0, 0))
    kern = functools.partial(_filter_kernel, tr=tr, L=L)
    return pl.pallas_call(
        kern,
        grid=(L // tr, ncol),
        in_specs=[
            pl.BlockSpec((tr, fh), lambda i, j: (i, 0)),
            mat(), vec(), mat(), vec(), mat(), vec(), vec(),
            pl.BlockSpec((fh, CH), lambda i, j: (0, j)),
            pl.BlockSpec((1, CH), lambda i, j: (0, 0)),
        ],
        out_specs=pl.BlockSpec((NT, tr, LANES), lambda i, j: (j, i, 0)),
        out_shape=jax.ShapeDtypeStruct((ncol * NT, L, LANES), F32),
        scratch_shapes=[pltpu.VMEM((tr, fh), F32)],
        compiler_params=_cparams("arbitrary", "arbitrary"),
        name="filter_mlp",
    )(z, w1p, b1.reshape(1, fh), w2, b2.reshape(1, fh), w3, b3.reshape(1, fh), freq.reshape(1, fh),
      w4, jnp.asarray(deltas).reshape(1, CH))


def _s1_kernel(m_ref, x_ref, o_ref, *, rows, s, strided, unroll):
    m = m_ref[...]
    x2 = x_ref.reshape(rows * s, LANES) if strided else None

    def body(j, carry):
        xj = x2[pl.ds(j, rows, stride=s), :] if strided else x_ref[j]
        o_ref[j] = jnp.dot(m, xj.astype(BF16), preferred_element_type=F32)
        return carry

    lax.fori_loop(0, s, body, 0, unroll=unroll)


def _s1(mat_bf, x, x_block, x_index, *, nb, nt, rows, s, strided, name):
    m, k = mat_bf.shape
    assert k == rows
    kern = functools.partial(_s1_kernel, rows=rows, s=s, strided=strided, unroll=8)
    return pl.pallas_call(
        kern,
        grid=(nb, nt, N2 // s),
        in_specs=[
            pl.BlockSpec((m, k), lambda bb, t, j: (0, 0)),
            pl.BlockSpec(x_block, x_index),
        ],
        out_specs=pl.BlockSpec((None, None, s, m, LANES), lambda bb, t, j: (bb, t, j, 0, 0)),
        out_shape=jax.ShapeDtypeStruct((nb, nt, N2, m, LANES), F32),
        compiler_params=_cparams("arbitrary", "arbitrary", "arbitrary"),
        name=name,
    )(mat_bf, x)


def _g_matrix(f2r, f2i, twr, twi):
    gr = f2r * twr - f2i * twi
    gi = f2r * twi + f2i * twr
    top = jnp.concatenate([gr, -gi], axis=1)
    bot = jnp.concatenate([gi, gr], axis=1)
    return jnp.concatenate([top, bot], axis=0).astype(BF16)


def _load_a(re_ref, im_ref, q, kc):
    def cat(ref):
        parts = [ref.at[t].reshape(N2 * kc, LANES)[pl.ds(q, N2, stride=kc), :] for t in range(ref.shape[0])]
        return parts[0] if len(parts) == 1 else jnp.concatenate(parts, axis=1)
    return jnp.concatenate([cat(re_ref), cat(im_ref)], axis=0).astype(BF16)


def _cat_tiles(ref, q):
    parts = [ref[t, q] for t in range(ref.shape[0])]
    return parts[0] if len(parts) == 1 else jnp.concatenate(parts, axis=1)


def _conv2_kernel(re_ref, im_ref, f2r_ref, f2i_ref, twr_ref, twi_ref, k_ref, o_ref, *, kc):
    f2r = f2r_ref[...]
    f2i = f2i_ref[...]
    for q in range(kc):
        g = _g_matrix(f2r, f2i, twr_ref[q:q + 1, :], twi_ref[q:q + 1, :])
        x = jnp.dot(g, _load_a(re_ref, im_ref, q, kc), preferred_element_type=F32)
        k = _cat_tiles(k_ref, q)
        xr, xi = x[:N2], x[N2:]
        kr, ki = k[:N2], k[N2:]
        y = jnp.concatenate([xr * kr - xi * ki, xr * ki + xi * kr], axis=0).astype(BF16)
        bk = lax.dot_general(g, y, (((0,), (0,)), ((), ())), preferred_element_type=F32)
        for t in range(o_ref.shape[0]):
            o_ref[t, q] = bk[:, t * LANES:(t + 1) * LANES]


def _conv2(a1, f2r, f2i, twr, twi, kspec, order, *, kc, ntb):
    b, nt, _, m, _ = a1.shape
    h = m // 2
    hb = h // kc
    kern = functools.partial(_conv2_kernel, kc=kc)
    return pl.pallas_call(
        kern,
        grid=(b, nt // ntb, hb),
        in_specs=[
            pl.BlockSpec((None, ntb, N2, kc, LANES), lambda bb, t, k: (bb, t, 0, k, 0)),
            pl.BlockSpec((None, ntb, N2, kc, LANES), lambda bb, t, k: (bb, t, 0, hb + k, 0)),
            pl.BlockSpec((N2, N2), lambda bb, t, k: (0, 0)),
            pl.BlockSpec((N2, N2), lambda bb, t, k: (0, 0)),
            pl.BlockSpec((kc, N2), lambda bb, t, k: (k, 0)),
            pl.BlockSpec((kc, N2), lambda bb, t, k: (k, 0)),
            pl.BlockSpec((None, ntb, kc, 2 * N2, LANES), lambda bb, t, k: (order, t, k, 0, 0)),
        ],
        out_specs=pl.BlockSpec((None, ntb, kc, 2 * N2, LANES), lambda bb, t, k: (bb, t, k, 0, 0)),
        out_shape=jax.ShapeDtypeStruct((b, nt, h, 2 * N2, LANES), F32),
        compiler_params=_cparams("arbitrary", "arbitrary", "arbitrary"),
        name="conv_stage2",
    )(a1, a1, f2r, f2i, twr, twi, kspec)


def _filt2_kernel(ref_ref, imf_ref, reb_ref, imb_ref, f2r_ref, f2i_ref, twr_ref, twi_ref, o_ref, *, kc):
    f2r = f2r_ref[...]
    f2i = f2i_ref[...]
    for q in range(kc):
        g = _g_matrix(f2r, f2i, twr_ref[q:q + 1, :], twi_ref[q:q + 1, :])
        xf = jnp.dot(g, _load_a(ref_ref, imf_ref, q, kc), preferred_element_type=F32)
        xb = jnp.dot(g, _load_a(reb_ref, imb_ref, q, kc), preferred_element_type=F32)
        k = jnp.concatenate([xf[:N2] + xb[:N2], xf[N2:] - xb[N2:]], axis=0)
        for t in range(o_ref.shape[0]):
            o_ref[t, q] = k[:, t * LANES:(t + 1) * LANES]


def _filt2(a1f, f2r, f2i, twr, twi, *, kc, ntb):
    _, _, _, m, _ = a1f.shape
    h = m // 2
    hb = h // kc
    ntg = NT // ntb
    kern = functools.partial(_filt2_kernel, kc=kc)
    blk = lambda im, d: pl.BlockSpec(
        (None, ntb, N2, kc, LANES), lambda o, t, k: (0, (2 * o + d) * ntg + t, 0, im * hb + k, 0))
    return pl.pallas_call(
        kern,
        grid=(2, ntg, hb),
        in_specs=[
            blk(0, 0), blk(1, 0), blk(0, 1), blk(1, 1),
            pl.BlockSpec((N2, N2), lambda o, t, k: (0, 0)),
            pl.BlockSpec((N2, N2), lambda o, t, k: (0, 0)),
            pl.BlockSpec((kc, N2), lambda o, t, k: (k, 0)),
            pl.BlockSpec((kc, N2), lambda o, t, k: (k, 0)),
        ],
        out_specs=pl.BlockSpec((None, ntb, kc, 2 * N2, LANES), lambda o, t, k: (o, t, k, 0, 0)),
        out_shape=jax.ShapeDtypeStruct((2, NT, h, 2 * N2, LANES), F32),
        compiler_params=_cparams("arbitrary", "arbitrary", "arbitrary"),
        name="filter_stage2",
    )(a1f, a1f, a1f, a1f, f2r, f2i, twr, twi)


def _s3_kernel(m_ref, b_ref, gate_ref, v_ref, skip_ref, o_ref, *, h, s, v_strided, out_strided, unroll):
    m = m_ref[...]
    skip = skip_ref[...]
    b2 = b_ref.reshape(h * 2 * s, LANES)
    gate2 = gate_ref.reshape(h * s, LANES)
    v2 = v_ref.reshape(h * s, LANES) if v_strided else None

    def body(j, carry):
        bj = b2[pl.ds(j, 2 * h, stride=s), :].astype(BF16)
        y = jnp.dot(m, bj, preferred_element_type=F32)
        vj = v2[pl.ds(j, h, stride=s), :] if v_strided else v_ref[j]
        zz = gate2[pl.ds(j, h, stride=s), :] * (y + skip * vj)
        if out_strided:
            o_ref[:, pl.ds(j, 1), :] = zz.reshape(h, 1, LANES)
        else:
            o_ref[j] = zz
        return carry

    lax.fori_loop(0, s, body, 0, unroll=unroll)


def _s3(f3i_bf, bspec, gate6, gate_group, v, v_group, skip4, order, *, s, out_natural):
    b, nt, h, _, _ = bspec.shape
    b6 = bspec.reshape(b, nt, h, 2, N2, LANES)
    nat = (None, None, None, h, s, LANES)
    v_strided = v_group is not None
    if v_strided:
        v_spec = pl.BlockSpec(nat, lambda bb, t, j: (bb, v_group, t, 0, j, 0))
    else:
        v_spec = pl.BlockSpec((None, None, s, h, LANES), lambda bb, t, j: (bb, t, j, 0, 0))
    if out_natural:
        out_spec = pl.BlockSpec((None, None, h, s, LANES), lambda bb, t, j: (bb, t, 0, j, 0))
        out_shape = jax.ShapeDtypeStruct((b, nt, h, N2, LANES), F32)
    else:
        out_spec = pl.BlockSpec((None, None, s, h, LANES), lambda bb, t, j: (bb, t, j, 0, 0))
        out_shape = jax.ShapeDtypeStruct((b, nt, N2, h, LANES), F32)
    kern = functools.partial(_s3_kernel, h=h, s=s, v_strided=v_strided, out_strided=out_natural, unroll=4)
    return pl.pallas_call(
        kern,
        grid=(b, nt, N2 // s),
        in_specs=[
            pl.BlockSpec((h, 2 * h), lambda bb, t, j: (0, 0)),
            pl.BlockSpec((None, None, h, 2, s, LANES), lambda bb, t, j: (bb, t, 0, 0, j, 0)),
            pl.BlockSpec(nat, lambda bb, t, j: (bb, gate_group, t, 0, j, 0)),
            v_spec,
            pl.BlockSpec((None, None, 1, LANES), lambda bb, t, j: (order, t, 0, 0)),
        ],
        out_specs=out_spec,
        out_shape=out_shape,
        compiler_params=_cparams("arbitrary", "arbitrary", "arbitrary"),
        name="conv_stage3",
    )(f3i_bf, b6, gate6, v, skip4)


def _chdft_kernel(u_ref, fc_ref, w_ref, o_ref):
    gd = FNET_GROUP_DIM
    tpg = gd // LANES
    fc = fc_ref[...]
    for g in range(FNET_GROUPS):
        u = jnp.concatenate([u_ref[g * tpg + t] for t in range(tpg)], axis=1).astype(BF16)
        z = jnp.dot(u, fc, preferred_element_type=F32)
        w = w_ref[g]
        zr = jnp.dot(z[:, :gd].astype(BF16), w, preferred_element_type=F32)
        zi = jnp.dot(z[:, gd:].astype(BF16), w, preferred_element_type=F32)
        for t in range(tpg):
            o_ref[0, g * tpg + t] = zr[:, t * LANES:(t + 1) * LANES]
            o_ref[1, g * tpg + t] = zi[:, t * LANES:(t + 1) * LANES]


def _chdft(p, fc_bf, fnet_w_bf, *, tm):
    b, _, nt, L, _ = p.shape
    gd = FNET_GROUP_DIM
    return pl.pallas_call(
        _chdft_kernel,
        grid=(b, L // tm),
        in_specs=[
            pl.BlockSpec((None, None, nt, tm, LANES), lambda bb, i: (bb, 4, 0, i, 0)),
            pl.BlockSpec((gd, 2 * gd), lambda bb, i: (0, 0)),
            pl.BlockSpec((FNET_GROUPS, gd, gd), lambda bb, i: (0, 0, 0)),
        ],
        out_specs=pl.BlockSpec((None, 2, nt, tm, LANES), lambda bb, i: (bb, 0, 0, i, 0)),
        out_shape=jax.ShapeDtypeStruct((b, 2, nt, L, LANES), F32),
        compiler_params=_cparams("arbitrary", "arbitrary"),
        name="fnet_chdft",
    )(p, fc_bf, fnet_w_bf)


def _fnet2_kernel(re_ref, im_ref, f2r_ref, f2i_ref, twr_ref, twi_ref, b_ref, o_ref, *, kc):
    f2r = f2r_ref[...]
    f2i = f2i_ref[...]
    for q in range(kc):
        twr = twr_ref[q:q + 1, :]
        twi = twi_ref[q:q + 1, :]
        g = jnp.concatenate([f2r * twr - f2i * twi, -(f2r * twi + f2i * twr)], axis=1).astype(BF16)
        fg = jnp.dot(g, _load_a(re_ref, im_ref, q, kc), preferred_element_type=F32)
        for t in range(o_ref.shape[0]):
            o_ref[t, :, q:q + 1, :] = (fg[:, t * LANES:(t + 1) * LANES] + b_ref[t]).reshape(N2, 1, LANES)


def _fnet2(a1, f2r, f2i, twr, twi, fnet_b, *, kc, ntb):
    b, nt, _, m, _ = a1.shape
    n1 = m // 2
    nb = n1 // kc
    kern = functools.partial(_fnet2_kernel, kc=kc)
    return pl.pallas_call(
        kern,
        grid=(b, nt // ntb, nb),
        in_specs=[
            pl.BlockSpec((None, ntb, N2, kc, LANES), lambda bb, t, k: (bb, t, 0, k, 0)),
            pl.BlockSpec((None, ntb, N2, kc, LANES), lambda bb, t, k: (bb, t, 0, nb + k, 0)),
            pl.BlockSpec((N2, N2), lambda bb, t, k: (0, 0)),
            pl.BlockSpec((N2, N2), lambda bb, t, k: (0, 0)),
            pl.BlockSpec((kc, N2), lambda bb, t, k: (k, 0)),
            pl.BlockSpec((kc, N2), lambda bb, t, k: (k, 0)),
            pl.BlockSpec((ntb, 1, LANES), lambda bb, t, k: (t, 0, 0)),
        ],
        out_specs=pl.BlockSpec((None, ntb, N2, kc, LANES), lambda bb, t, k: (bb, t, 0, k, 0)),
        out_shape=jax.ShapeDtypeStruct((b, nt, N2, n1, LANES), F32),
        compiler_params=_cparams("arbitrary", "arbitrary", "arbitrary"),
        name="fnet_stage2",
    )(a1, a1, f2r, f2i, twr, twi, fnet_b.reshape(nt, 1, LANES))


def _outproj_kernel(zz_ref, zhy_ref, yg_ref, zfn_ref, x_ref, nhy_ref, nfn_ref, w_ref, fin_ref,
                    o_ref, cat_ref):
    def silu(z):
        return z / (1.0 + jnp.exp(-z))

    def branch(a_ref, z_ref, g_ref, col0):
        nt = a_ref.shape[0]
        a = [a_ref[t] * silu(z_ref[t]) for t in range(nt)]
        ss = sum(jnp.sum(at * at, axis=-1, keepdims=True) for at in a)
        rs = lax.rsqrt(ss * (1.0 / (nt * LANES)) + EPS)
        for t in range(nt):
            lo = col0 + t * LANES
            cat_ref[:, lo:lo + LANES] = (a[t] * rs * g_ref[:, t * LANES:(t + 1) * LANES]).astype(BF16)

    branch(zz_ref, zhy_ref, nhy_ref, 0)
    branch(yg_ref, zfn_ref, nfn_ref, zz_ref.shape[0] * LANES)
    y = x_ref[...] + jnp.dot(cat_ref[...], w_ref[...], preferred_element_type=F32)
    ms = jnp.mean(y * y, axis=-1, keepdims=True)
    o_ref[...] = y * lax.rsqrt(ms + EPS) * fin_ref[...]


def _outproj(zz, p, yg, x, norm_hy, norm_fn, w_out_bf, final_norm, *, tm):
    b, L, d = x.shape
    c = CH
    vecc = lambda: pl.BlockSpec((1, c), lambda bb, i: (0, 0))
    tiles = lambda: pl.BlockSpec((None, NT, tm, LANES), lambda bb, i: (bb, 0, i, 0))
    group = lambda g: pl.BlockSpec((None, None, NT, tm, LANES), lambda bb, i: (bb, g, 0, i, 0))
    return pl.pallas_call(
        _outproj_kernel,
        grid=(b, L // tm),
        in_specs=[
            tiles(), group(3), tiles(), group(5),
            pl.BlockSpec((None, tm, d), lambda bb, i: (bb, i, 0)),
            vecc(), vecc(),
            pl.BlockSpec((2 * c, d), lambda bb, i: (0, 0)),
            pl.BlockSpec((1, d), lambda bb, i: (0, 0)),
        ],
        out_specs=pl.BlockSpec((None, tm, d), lambda bb, i: (bb, i, 0)),
        out_shape=jax.ShapeDtypeStruct((b, L, d), F32),
        scratch_shapes=[pltpu.VMEM((tm, 2 * c), BF16)],
        compiler_params=_cparams("arbitrary", "arbitrary"),
        name="outproj",
    )(zz, p, yg, p, x, norm_hy.reshape(1, c), norm_fn.reshape(1, c), w_out_bf, final_norm.reshape(1, d))


def _trunk(x, norm_g, w_in_bf, conv_w, conv_b, w1p, b1, w2, b2, w3, b3, w4, freq, skip,
           fnet_w_bf, fnet_b, norm_hy, norm_fn, w_out_bf, final_norm):
    b, L, d = x.shape
    n = 2 * L
    h = n // N2 // 2
    n1f = L // N2
    s = min(N2, 4096 // h)
    kc = 8
    ntb = 4

    f1, f3, twr, twi = (jnp.asarray(t) for t in _conv_tables(L))
    f2r, f2i = (jnp.asarray(t) for t in _f2_tables())
    fn1, ftwr, ftwi, fc = (jnp.asarray(t) for t in _fnet_tables(L))
    f1_bf, f3i_bf, fn1_bf, fc_bf = (t.astype(BF16) for t in (f1, f3, fn1, fc))

    p = _inproj(x, norm_g, w_in_bf, conv_w, conv_b, tm=min(512, L))
    p6 = p.reshape(b, N_PROJ_GROUPS, NT, h, N2, LANES)
    nat = (None, None, None, h, s, LANES)

    filt = _filters(L, w1p, b1, w2, b2, w3, b3, freq, w4, tr=min(512, L))
    a1f = _s1(f1_bf, filt.reshape(1, 1, 4 * NT, h, N2, LANES), nat, lambda bb, t, j: (0, 0, t, 0, j, 0),
              nb=1, nt=4 * NT, rows=h, s=s, strided=True, name="filter_stage1")
    kspec = _filt2(a1f, f2r, f2i, twr, twi, kc=kc, ntb=2)

    skip4 = skip.reshape(2, NT, 1, LANES)

    a1 = _s1(f1_bf, p6, nat, lambda bb, t, j: (bb, 2, t, 0, j, 0), nb=b, nt=NT, rows=h, s=s, strided=True,
             name="conv_stage1")
    bsp = _conv2(a1, f2r, f2i, twr, twi, kspec, 0, kc=kc, ntb=ntb)
    zz1 = _s3(f3i_bf, bsp, p6, 0, p6, 2, skip4, 0, s=s, out_natural=False)
    a1 = _s1(f1_bf, zz1, (None, None, s, h, LANES), lambda bb, t, j: (bb, t, j, 0, 0), nb=b, nt=NT, rows=h, s=s,
             strided=False, name="conv_stage1")
    bsp = _conv2(a1, f2r, f2i, twr, twi, kspec, 1, kc=kc, ntb=ntb)
    zz2 = _s3(f3i_bf, bsp, p6, 1, zz1, None, skip4, 1, s=s, out_natural=True)

    z = _chdft(p, fc_bf, fnet_w_bf, tm=min(512, L))
    sf = min(N2, 4096 // (2 * n1f))
    a1z = _s1(fn1_bf, z.reshape(b, 2, NT, n1f, N2, LANES), (None, 2, None, n1f, sf, LANES),
              lambda bb, t, j: (bb, 0, t, 0, j, 0), nb=b, nt=NT, rows=2 * n1f, s=sf, strided=True,
              name="fnet_stage1")
    yg = _fnet2(a1z, f2r, f2i, ftwr, ftwi, fnet_b, kc=kc, ntb=ntb)

    return _outproj(zz2.reshape(b, NT, L, LANES), p, yg.reshape(b, NT, L, LANES), x, norm_hy, norm_fn,
                    w_out_bf, final_norm, tm=min(512, L))


def kernel(x_prompt, x_sample, norm_g, w_in, conv_w, conv_b, filt_w1, filt_b1, filt_w2, filt_b2,
           filt_w3, filt_b3, filt_w4, filt_freq, filt_skip, fnet_w, fnet_b, norm_hy, norm_fn,
           w_out, final_norm):
    l = 0
    w1p = jnp.zeros((FILTER_HIDDEN, FILTER_HIDDEN), F32).at[:FILTER_EMB].set(filt_w1[l])
    args = (norm_g[l], w_in[l].astype(BF16), conv_w[l], conv_b[l], w1p, filt_b1[l], filt_w2[l],
            filt_b2[l], filt_w3[l], filt_b3[l], filt_w4[l], filt_freq[l], filt_skip[l],
            fnet_w[l].astype(BF16), fnet_b[l], norm_hy[l], norm_fn[l], w_out[l].astype(BF16),
            final_norm)
    return (_trunk(x_prompt, *args), _trunk(x_sample, *args))
```

```python
import functools
import math

import numpy as np
import jax
import jax.numpy as jnp
from jax import lax
from jax.experimental import pallas as pl
from jax.experimental.pallas import tpu as pltpu

F32 = jnp.float32
BF16 = jnp.bfloat16

D_MODEL = 2048
D_HYENA = D_MODEL // 2
D_FNET = D_MODEL - D_HYENA
HYENA_HEADS = 8
HYENA_HEAD_DIM = D_HYENA // HYENA_HEADS
FNET_GROUPS = 4
FNET_GROUP_DIM = D_FNET // FNET_GROUPS
FILTER_EMB = 33
FILTER_HIDDEN = 64
MIN_DECAY = math.log(1e-2) / 1.5
MAX_DECAY = math.log(1e-2) / 0.3
EPS = 1e-6

LANES = 128
N_PROJ_GROUPS = 6
CH = 1024
NT = CH // LANES
N2 = 128
HALO = 16
VMEM_LIMIT = 52 * 1024 * 1024


def _cparams(*sem):
    return pltpu.CompilerParams(dimension_semantics=sem, vmem_limit_bytes=VMEM_LIMIT)


def _conv_tables(L):
    n = 2 * L
    n1 = n // N2
    h = n1 // 2
    k1 = np.arange(h, dtype=np.float64)
    ang1 = 2.0 * np.pi * np.outer(k1 + 0.5, np.arange(h)) / n1
    f1 = np.concatenate([np.cos(ang1), -np.sin(ang1)], axis=0)
    f3 = (2.0 / n) * np.stack([np.cos(ang1).T, -np.sin(ang1).T], axis=-1).reshape(h, 2 * h)
    angw = 2.0 * np.pi * np.outer(k1 + 0.5, np.arange(N2)) / n
    return (f1.astype(np.float32), f3.astype(np.float32),
            np.cos(angw).astype(np.float32), (-np.sin(angw)).astype(np.float32))


def _f2_tables():
    ang = 2.0 * np.pi * np.outer(np.arange(N2), np.arange(N2)) / N2
    return np.cos(ang).astype(np.float32), (-np.sin(ang)).astype(np.float32)


def _fnet_tables(L):
    n1 = L // N2
    ang1 = 2.0 * np.pi * np.outer(np.arange(n1), np.arange(n1)) / n1
    c, s = np.cos(ang1), np.sin(ang1)
    fn1 = np.block([[c, s], [-s, c]])
    angw = 2.0 * np.pi * np.outer(np.arange(n1), np.arange(N2)) / L
    scale = 1.0 / math.sqrt(L * FNET_GROUP_DIM)
    gd = FNET_GROUP_DIM
    angc = 2.0 * np.pi * np.outer(np.arange(gd), np.arange(gd)) / gd
    fc = np.concatenate([np.cos(angc), -np.sin(angc)], axis=1)
    return (fn1.astype(np.float32), (scale * np.cos(angw)).astype(np.float32),
            (-scale * np.sin(angw)).astype(np.float32), fc.astype(np.float32))


def _positional_features(L):
    bands = (FILTER_EMB - 1) // 2
    t = np.linspace(0.0, 1.0, L)
    w = 2.0 * np.pi * np.arange(L) / L
    f = np.linspace(1e-4, bands - 1, bands)
    fw = w[:, None] * f[None, :]
    z = np.concatenate([t[:, None], np.cos(fw), -np.sin(fw)], axis=-1)
    zp = np.zeros((L, FILTER_HIDDEN), np.float64)
    zp[:, :FILTER_EMB] = z
    return zp.astype(np.float32)


def _store_tiles(o_ref, val):
    for t in range(val.shape[-1] // LANES):
        o_ref[t] = val[:, t * LANES:(t + 1) * LANES]


def _inproj_kernel(xm_ref, xt_ref, xb_ref, g_ref, w_ref, cw_ref, cb_ref, o_ref, hn_ref, *, tm, n_conv):
    i = pl.program_id(1)
    j = pl.program_id(2)
    last_i = pl.num_programs(1) - 1

    @pl.when(j == 0)
    def _():
        g = g_ref[...]

        def nrm(x):
            ms = jnp.mean(x * x, axis=-1, keepdims=True)
            return x * lax.rsqrt(ms + EPS) * g

        hn_ref[HALO:HALO + tm, :] = nrm(xm_ref[...]).astype(BF16)
        top = jnp.where(i > 0, nrm(xt_ref[...]), 0.0)
        bot = jnp.where(i < last_i, nrm(xb_ref[...]), 0.0)
        hn_ref[0:HALO, :] = top.astype(BF16)
        hn_ref[HALO + tm:, :] = bot.astype(BF16)

    @pl.when(j < n_conv)
    def _():
        rows = tm + 2 * HALO
        p = jnp.dot(hn_ref[...], w_ref[...], preferred_element_type=F32)
        prev = pltpu.roll(p, 1, axis=0)
        nxt = pltpu.roll(p, rows - 1, axis=0)
        cw = cw_ref[...]
        u = prev * cw[0:1, :] + p * cw[1:2, :] + nxt * cw[2:3, :] + cb_ref[...]
        _store_tiles(o_ref, u[HALO:HALO + tm, :])

    @pl.when(j >= n_conv)
    def _():
        _store_tiles(o_ref, jnp.dot(hn_ref[HALO:HALO + tm, :], w_ref[...], preferred_element_type=F32))


def _inproj(x, norm_g, w_in_bf, conv_w, conv_b, *, tm):
    b, L, d = x.shape
    n_conv = 3
    rt = tm // HALO
    nhalo = L // HALO
    kern = functools.partial(_inproj_kernel, tm=tm, n_conv=n_conv)
    return pl.pallas_call(
        kern,
        grid=(b, L // tm, N_PROJ_GROUPS),
        in_specs=[
            pl.BlockSpec((None, tm, d), lambda bb, i, j: (bb, i, 0)),
            pl.BlockSpec((None, HALO, d), lambda bb, i, j: (bb, jnp.maximum(i * rt - 1, 0), 0)),
            pl.BlockSpec((None, HALO, d), lambda bb, i, j: (bb, jnp.minimum((i + 1) * rt, nhalo - 1), 0)),
            pl.BlockSpec((1, d), lambda bb, i, j: (0, 0)),
            pl.BlockSpec((d, CH), lambda bb, i, j: (0, j)),
            pl.BlockSpec((3, CH), lambda bb, i, j: (0, jnp.minimum(j, n_conv - 1))),
            pl.BlockSpec((1, CH), lambda bb, i, j: (0, jnp.minimum(j, n_conv - 1))),
        ],
        out_specs=pl.BlockSpec((None, None, NT, tm, LANES), lambda bb, i, j: (bb, j, 0, i, 0)),
        out_shape=jax.ShapeDtypeStruct((b, N_PROJ_GROUPS, NT, L, LANES), F32),
        scratch_shapes=[pltpu.VMEM((tm + 2 * HALO, d), BF16)],
        compiler_params=_cparams("arbitrary", "arbitrary", "arbitrary"),
        name="inproj",
    )(x, x, x, norm_g.reshape(1, d), w_in_bf, conv_w, conv_b.reshape(1, -1))


def _filter_kernel(z_ref, w1_ref, b1_ref, w2_ref, b2_ref, w3_ref, b3_ref, fr_ref, w4_ref, dl_ref,
                   o_ref, h_ref, *, tr, L):
    i = pl.program_id(0)
    j = pl.program_id(1)
    hp = lax.Precision.HIGHEST

    @pl.when(j == 0)
    def _():
        fr = fr_ref[...]
        h = jnp.sin(fr * (jnp.dot(z_ref[...], w1_ref[...], precision=hp, preferred_element_type=F32) + b1_ref[...]))
        h = jnp.sin(fr * (jnp.dot(h, w2_ref[...], precision=hp, preferred_element_type=F32) + b2_ref[...]))
        h = jnp.sin(fr * (jnp.dot(h, w3_ref[...], precision=hp, preferred_element_type=F32) + b3_ref[...]))
        hi = h.astype(BF16)
        lo = (h - hi.astype(F32)).astype(BF16)
        h_ref[...] = jnp.concatenate([hi, lo, hi], axis=1)

    f = jnp.dot(h_ref[...], w4_ref[...], preferred_element_type=F32)
    row = i * tr + lax.broadcasted_iota(jnp.int32, (tr, 1), 0)
    t = row.astype(F32) * (1.0 / (L - 1))
    f = f * jnp.exp(-t * jnp.abs(dl_ref[...]))
    drop = jnp.logical_and(row == 0, (j % 2) == 1)
    _store_tiles(o_ref, jnp.where(drop, 0.0, f))


def _filters(L, w1p, b1, w2, b2, w3, b3, freq, w4, *, tr):
    z = jnp.asarray(_positional_features(L))
    deltas = np.tile(np.linspace(MIN_DECAY, MAX_DECAY, HYENA_HEAD_DIM), HYENA_HEADS).astype(np.float32)
    fh = FILTER_HIDDEN
    ncol = w4.shape[1] // CH
    vec = lambda: pl.BlockSpec((1, fh), lambda i, j: (0, 0))
    mat = lambda: pl.BlockSpec((fh, fh), lambda i, j: (0, 0))
    kern = functools.partial(_filter_kernel, tr=tr, L=L)
    w4_hi = w4.astype(BF16)
    w4_lo = (w4 - w4_hi.astype(F32)).astype(BF16)
    w4 = jnp.concatenate([w4_hi, w4_hi, w4_lo], axis=0)
    return pl.pallas_call(
        kern,
        grid=(L // tr, ncol),
        in_specs=[
            pl.BlockSpec((tr, fh), lambda i, j: (i, 0)),
            mat(), vec(), mat(), vec(), mat(), vec(), vec(),
            pl.BlockSpec((3 * fh, CH), lambda i, j: (0, j)),
            pl.BlockSpec((1, CH), lambda i, j: (0, 0)),
        ],
        out_specs=pl.BlockSpec((NT, tr, LANES), lambda i, j: (j, i, 0)),
        out_shape=jax.ShapeDtypeStruct((ncol * NT, L, LANES), F32),
        scratch_shapes=[pltpu.VMEM((tr, 3 * fh), BF16)],
        compiler_params=_cparams("arbitrary", "arbitrary"),
        name="filter_mlp",
    )(z, w1p, b1.reshape(1, fh), w2, b2.reshape(1, fh), w3, b3.reshape(1, fh), freq.reshape(1, fh),
      w4, jnp.asarray(deltas).reshape(1, CH))


def _s1_kernel(m_ref, x_ref, o_ref, *, rows, s, strided, unroll):
    m = m_ref[...]
    x2 = x_ref.reshape(rows * s, LANES) if strided else None

    def body(j, carry):
        xj = x2[pl.ds(j, rows, stride=s), :] if strided else x_ref[j]
        o_ref[j] = jnp.dot(m, xj.astype(BF16), preferred_element_type=F32)
        return carry

    lax.fori_loop(0, s, body, 0, unroll=unroll)


def _s1(mat_bf, x, x_block, x_index, *, nb, nt, rows, s, strided, name):
    m, k = mat_bf.shape
    assert k == rows
    kern = functools.partial(_s1_kernel, rows=rows, s=s, strided=strided, unroll=8)
    return pl.pallas_call(
        kern,
        grid=(nb, nt, N2 // s),
        in_specs=[
            pl.BlockSpec((m, k), lambda bb, t, j: (0, 0)),
            pl.BlockSpec(x_block, x_index),
        ],
        out_specs=pl.BlockSpec((None, None, s, m, LANES), lambda bb, t, j: (bb, t, j, 0, 0)),
        out_shape=jax.ShapeDtypeStruct((nb, nt, N2, m, LANES), F32),
        compiler_params=_cparams("arbitrary", "arbitrary", "arbitrary"),
        name=name,
    )(mat_bf, x)


def _g_matrix(f2r, f2i, twr, twi):
    gr = f2r * twr - f2i * twi
    gi = f2r * twi + f2i * twr
    top = jnp.concatenate([gr, -gi], axis=1)
    bot = jnp.concatenate([gi, gr], axis=1)
    return jnp.concatenate([top, bot], axis=0).astype(BF16)


def _load_a(re_ref, im_ref, q, kc):
    def cat(ref):
        parts = [ref.at[t].reshape(N2 * kc, LANES)[pl.ds(q, N2, stride=kc), :] for t in range(ref.shape[0])]
        return parts[0] if len(parts) == 1 else jnp.concatenate(parts, axis=1)
    return jnp.concatenate([cat(re_ref), cat(im_ref)], axis=0).astype(BF16)


def _cat_tiles(ref, q):
    parts = [ref[t, q] for t in range(ref.shape[0])]
    return parts[0] if len(parts) == 1 else jnp.concatenate(parts, axis=1)


def _conv2_kernel(re_ref, im_ref, f2r_ref, f2i_ref, twr_ref, twi_ref, k_ref, o_ref, *, kc):
    f2r = f2r_ref[...]
    f2i = f2i_ref[...]
    for q in range(kc):
        g = _g_matrix(f2r, f2i, twr_ref[q:q + 1, :], twi_ref[q:q + 1, :])
        x = jnp.dot(g, _load_a(re_ref, im_ref, q, kc), preferred_element_type=F32)
        k = _cat_tiles(k_ref, q)
        xr, xi = x[:N2], x[N2:]
        kr, ki = k[:N2], k[N2:]
        y = jnp.concatenate([xr * kr - xi * ki, xr * ki + xi * kr], axis=0).astype(BF16)
        bk = lax.dot_general(g, y, (((0,), (0,)), ((), ())), preferred_element_type=F32)
        for t in range(o_ref.shape[0]):
            o_ref[t, q] = bk[:, t * LANES:(t + 1) * LANES]


def _conv2(a1, f2r, f2i, twr, twi, kspec, order, *, kc, ntb):
    b, nt, _, m, _ = a1.shape
    h = m // 2
    hb = h // kc
    kern = functools.partial(_conv2_kernel, kc=kc)
    return pl.pallas_call(
        kern,
        grid=(b, nt // ntb, hb),
        in_specs=[
            pl.BlockSpec((None, ntb, N2, kc, LANES), lambda bb, t, k: (bb, t, 0, k, 0)),
            pl.BlockSpec((None, ntb, N2, kc, LANES), lambda bb, t, k: (bb, t, 0, hb + k, 0)),
            pl.BlockSpec((N2, N2), lambda bb, t, k: (0, 0)),
            pl.BlockSpec((N2, N2), lambda bb, t, k: (0, 0)),
            pl.BlockSpec((kc, N2), lambda bb, t, k: (k, 0)),
            pl.BlockSpec((kc, N2), lambda bb, t, k: (k, 0)),
            pl.BlockSpec((None, ntb, kc, 2 * N2, LANES), lambda bb, t, k: (order, t, k, 0, 0)),
        ],
        out_specs=pl.BlockSpec((None, ntb, kc, 2 * N2, LANES), lambda bb, t, k: (bb, t, k, 0, 0)),
        out_shape=jax.ShapeDtypeStruct((b, nt, h, 2 * N2, LANES), F32),
        compiler_params=_cparams("arbitrary", "arbitrary", "arbitrary"),
        name="conv_stage2",
    )(a1, a1, f2r, f2i, twr, twi, kspec)


def _filt2_kernel(ref_ref, imf_ref, reb_ref, imb_ref, f2r_ref, f2i_ref, twr_ref, twi_ref, o_ref, *, kc):
    f2r = f2r_ref[...]
    f2i = f2i_ref[...]
    for q in range(kc):
        g = _g_matrix(f2r, f2i, twr_ref[q:q + 1, :], twi_ref[q:q + 1, :])
        xf = jnp.dot(g, _load_a(ref_ref, imf_ref, q, kc), preferred_element_type=F32)
        xb = jnp.dot(g, _load_a(reb_ref, imb_ref, q, kc), preferred_element_type=F32)
        k = jnp.concatenate([xf[:N2] + xb[:N2], xf[N2:] - xb[N2:]], axis=0)
        for t in range(o_ref.shape[0]):
            o_ref[t, q] = k[:, t * LANES:(t + 1) * LANES]


def _filt2(a1f, f2r, f2i, twr, twi, *, kc, ntb):
    _, _, _, m, _ = a1f.shape
    h = m // 2
    hb = h // kc
    ntg = NT // ntb
    kern = functools.partial(_filt2_kernel, kc=kc)
    blk = lambda im, d: pl.BlockSpec(
        (None, ntb, N2, kc, LANES), lambda o, t, k: (0, (2 * o + d) * ntg + t, 0, im * hb + k, 0))
    return pl.pallas_call(
        kern,
        grid=(2, ntg, hb),
        in_specs=[
            blk(0, 0), blk(1, 0), blk(0, 1), blk(1, 1),
            pl.BlockSpec((N2, N2), lambda o, t, k: (0, 0)),
            pl.BlockSpec((N2, N2), lambda o, t, k: (0, 0)),
            pl.BlockSpec((kc, N2), lambda o, t, k: (k, 0)),
            pl.BlockSpec((kc, N2), lambda o, t, k: (k, 0)),
        ],
        out_specs=pl.BlockSpec((None, ntb, kc, 2 * N2, LANES), lambda o, t, k: (o, t, k, 0, 0)),
        out_shape=jax.ShapeDtypeStruct((2, NT, h, 2 * N2, LANES), F32),
        compiler_params=_cparams("arbitrary", "arbitrary", "arbitrary"),
        name="filter_stage2",
    )(a1f, a1f, a1f, a1f, f2r, f2i, twr, twi)


def _s3_kernel(m_ref, b_ref, gate_ref, v_ref, skip_ref, o_ref, *, h, s, v_strided, out_strided, unroll):
    m = m_ref[...]
    skip = skip_ref[...]
    b2 = b_ref.reshape(h * 2 * s, LANES)
    gate2 = gate_ref.reshape(h * s, LANES)
    v2 = v_ref.reshape(h * s, LANES) if v_strided else None

    def body(j, carry):
        bj = b2[pl.ds(j, 2 * h, stride=s), :].astype(BF16)
        y = jnp.dot(m, bj, preferred_element_type=F32)
        vj = v2[pl.ds(j, h, stride=s), :] if v_strided else v_ref[j]
        zz = gate2[pl.ds(j, h, stride=s), :] * (y + skip * vj)
        if out_strided:
            o_ref[:, pl.ds(j, 1), :] = zz.reshape(h, 1, LANES)
        else:
            o_ref[j] = zz
        return carry

    lax.fori_loop(0, s, body, 0, unroll=unroll)


def _s3(f3i_bf, bspec, gate6, gate_group, v, v_group, skip4, order, *, s, out_natural):
    b, nt, h, _, _ = bspec.shape
    b6 = bspec.reshape(b, nt, h, 2, N2, LANES)
    nat = (None, None, None, h, s, LANES)
    v_strided = v_group is not None
    if v_strided:
        v_spec = pl.BlockSpec(nat, lambda bb, t, j: (bb, v_group, t, 0, j, 0))
    else:
        v_spec = pl.BlockSpec((None, None, s, h, LANES), lambda bb, t, j: (bb, t, j, 0, 0))
    if out_natural:
        out_spec = pl.BlockSpec((None, None, h, s, LANES), lambda bb, t, j: (bb, t, 0, j, 0))
        out_shape = jax.ShapeDtypeStruct((b, nt, h, N2, LANES), F32)
    else:
        out_spec = pl.BlockSpec((None, None, s, h, LANES), lambda bb, t, j: (bb, t, j, 0, 0))
        out_shape = jax.ShapeDtypeStruct((b, nt, N2, h, LANES), F32)
    kern = functools.partial(_s3_kernel, h=h, s=s, v_strided=v_strided, out_strided=out_natural, unroll=4)
    return pl.pallas_call(
        kern,
        grid=(b, nt, N2 // s),
        in_specs=[
            pl.BlockSpec((h, 2 * h), lambda bb, t, j: (0, 0)),
            pl.BlockSpec((None, None, h, 2, s, LANES), lambda bb, t, j: (bb, t, 0, 0, j, 0)),
            pl.BlockSpec(nat, lambda bb, t, j: (bb, gate_group, t, 0, j, 0)),
            v_spec,
            pl.BlockSpec((None, None, 1, LANES), lambda bb, t, j: (order, t, 0, 0)),
        ],
        out_specs=out_spec,
        out_shape=out_shape,
        compiler_params=_cparams("arbitrary", "arbitrary", "arbitrary"),
        name="conv_stage3",
    )(f3i_bf, b6, gate6, v, skip4)


def _chdft_kernel(u_ref, fc_ref, w_ref, o_ref):
    gd = FNET_GROUP_DIM
    tpg = gd // LANES
    fc = fc_ref[...]
    for g in range(FNET_GROUPS):
        u = jnp.concatenate([u_ref[g * tpg + t] for t in range(tpg)], axis=1).astype(BF16)
        z = jnp.dot(u, fc, preferred_element_type=F32)
        w = w_ref[g]
        zr = jnp.dot(z[:, :gd].astype(BF16), w, preferred_element_type=F32)
        zi = jnp.dot(z[:, gd:].astype(BF16), w, preferred_element_type=F32)
        for t in range(tpg):
            o_ref[0, g * tpg + t] = zr[:, t * LANES:(t + 1) * LANES]
            o_ref[1, g * tpg + t] = zi[:, t * LANES:(t + 1) * LANES]


def _chdft(p, fc_bf, fnet_w_bf, *, tm):
    b, _, nt, L, _ = p.shape
    gd = FNET_GROUP_DIM
    return pl.pallas_call(
        _chdft_kernel,
        grid=(b, L // tm),
        in_specs=[
            pl.BlockSpec((None, None, nt, tm, LANES), lambda bb, i: (bb, 4, 0, i, 0)),
            pl.BlockSpec((gd, 2 * gd), lambda bb, i: (0, 0)),
            pl.BlockSpec((FNET_GROUPS, gd, gd), lambda bb, i: (0, 0, 0)),
        ],
        out_specs=pl.BlockSpec((None, 2, nt, tm, LANES), lambda bb, i: (bb, 0, 0, i, 0)),
        out_shape=jax.ShapeDtypeStruct((b, 2, nt, L, LANES), F32),
        compiler_params=_cparams("arbitrary", "arbitrary"),
        name="fnet_chdft",
    )(p, fc_bf, fnet_w_bf)


def _fnet2_kernel(re_ref, im_ref, f2r_ref, f2i_ref, twr_ref, twi_ref, b_ref, o_ref, *, kc):
    f2r = f2r_ref[...]
    f2i = f2i_ref[...]
    for q in range(kc):
        twr = twr_ref[q:q + 1, :]
        twi = twi_ref[q:q + 1, :]
        g = jnp.concatenate([f2r * twr - f2i * twi, -(f2r * twi + f2i * twr)], axis=1).astype(BF16)
        fg = jnp.dot(g, _load_a(re_ref, im_ref, q, kc), preferred_element_type=F32)
        for t in range(o_ref.shape[0]):
            o_ref[t, :, q:q + 1, :] = (fg[:, t * LANES:(t + 1) * LANES] + b_ref[t]).reshape(N2, 1, LANES)


def _fnet2(a1, f2r, f2i, twr, twi, fnet_b, *, kc, ntb):
    b, nt, _, m, _ = a1.shape
    n1 = m // 2
    nb = n1 // kc
    kern = functools.partial(_fnet2_kernel, kc=kc)
    return pl.pallas_call(
        kern,
        grid=(b, nt // ntb, nb),
        in_specs=[
            pl.BlockSpec((None, ntb, N2, kc, LANES), lambda bb, t, k: (bb, t, 0, k, 0)),
            pl.BlockSpec((None, ntb, N2, kc, LANES), lambda bb, t, k: (bb, t, 0, nb + k, 0)),
            pl.BlockSpec((N2, N2), lambda bb, t, k: (0, 0)),
            pl.BlockSpec((N2, N2), lambda bb, t, k: (0, 0)),
            pl.BlockSpec((kc, N2), lambda bb, t, k: (k, 0)),
            pl.BlockSpec((kc, N2), lambda bb, t, k: (k, 0)),
            pl.BlockSpec((ntb, 1, LANES), lambda bb, t, k: (t, 0, 0)),
        ],
        out_specs=pl.BlockSpec((None, ntb, N2, kc, LANES), lambda bb, t, k: (bb, t, 0, k, 0)),
        out_shape=jax.ShapeDtypeStruct((b, nt, N2, n1, LANES), F32),
        compiler_params=_cparams("arbitrary", "arbitrary", "arbitrary"),
        name="fnet_stage2",
    )(a1, a1, f2r, f2i, twr, twi, fnet_b.reshape(nt, 1, LANES))


def _outproj_kernel(zz_ref, zhy_ref, yg_ref, zfn_ref, x_ref, nhy_ref, nfn_ref, w_ref, fin_ref,
                    o_ref, cat_ref):
    def silu(z):
        return z / (1.0 + jnp.exp(-z))

    def branch(a_ref, z_ref, g_ref, col0):
        nt = a_ref.shape[0]
        a = [a_ref[t] * silu(z_ref[t]) for t in range(nt)]
        ss = sum(jnp.sum(at * at, axis=-1, keepdims=True) for at in a)
        rs = lax.rsqrt(ss * (1.0 / (nt * LANES)) + EPS)
        for t in range(nt):
            lo = col0 + t * LANES
            cat_ref[:, lo:lo + LANES] = (a[t] * rs * g_ref[:, t * LANES:(t + 1) * LANES]).astype(BF16)

    branch(zz_ref, zhy_ref, nhy_ref, 0)
    branch(yg_ref, zfn_ref, nfn_ref, zz_ref.shape[0] * LANES)
    y = x_ref[...] + jnp.dot(cat_ref[...], w_ref[...], preferred_element_type=F32)
    ms = jnp.mean(y * y, axis=-1, keepdims=True)
    o_ref[...] = y * lax.rsqrt(ms + EPS) * fin_ref[...]


def _outproj(zz, p, yg, x, norm_hy, norm_fn, w_out_bf, final_norm, *, tm):
    b, L, d = x.shape
    c = CH
    vecc = lambda: pl.BlockSpec((1, c), lambda bb, i: (0, 0))
    tiles = lambda: pl.BlockSpec((None, NT, tm, LANES), lambda bb, i: (bb, 0, i, 0))
    group = lambda g: pl.BlockSpec((None, None, NT, tm, LANES), lambda bb, i: (bb, g, 0, i, 0))
    return pl.pallas_call(
        _outproj_kernel,
        grid=(b, L // tm),
        in_specs=[
            tiles(), group(3), tiles(), group(5),
            pl.BlockSpec((None, tm, d), lambda bb, i: (bb, i, 0)),
            vecc(), vecc(),
            pl.BlockSpec((2 * c, d), lambda bb, i: (0, 0)),
            pl.BlockSpec((1, d), lambda bb, i: (0, 0)),
        ],
        out_specs=pl.BlockSpec((None, tm, d), lambda bb, i: (bb, i, 0)),
        out_shape=jax.ShapeDtypeStruct((b, L, d), F32),
        scratch_shapes=[pltpu.VMEM((tm, 2 * c), BF16)],
        compiler_params=_cparams("arbitrary", "arbitrary"),
        name="outproj",
    )(zz, p, yg, p, x, norm_hy.reshape(1, c), norm_fn.reshape(1, c), w_out_bf, final_norm.reshape(1, d))


def _trunk(x, norm_g, w_in_bf, conv_w, conv_b, w1p, b1, w2, b2, w3, b3, w4, freq, skip,
           fnet_w_bf, fnet_b, norm_hy, norm_fn, w_out_bf, final_norm):
    b, L, d = x.shape
    n = 2 * L
    h = n // N2 // 2
    n1f = L // N2
    s = min(N2, 4096 // h)
    kc = 8
    ntb = 4

    f1, f3, twr, twi = (jnp.asarray(t) for t in _conv_tables(L))
    f2r, f2i = (jnp.asarray(t) for t in _f2_tables())
    fn1, ftwr, ftwi, fc = (jnp.asarray(t) for t in _fnet_tables(L))
    f1_bf, f3i_bf, fn1_bf, fc_bf = (t.astype(BF16) for t in (f1, f3, fn1, fc))

    p = _inproj(x, norm_g, w_in_bf, conv_w, conv_b, tm=min(1024, L))
    p6 = p.reshape(b, N_PROJ_GROUPS, NT, h, N2, LANES)
    nat = (None, None, None, h, s, LANES)

    filt = _filters(L, w1p, b1, w2, b2, w3, b3, freq, w4, tr=min(512, L))
    a1f = _s1(f1_bf, filt.reshape(1, 1, 4 * NT, h, N2, LANES), nat, lambda bb, t, j: (0, 0, t, 0, j, 0),
              nb=1, nt=4 * NT, rows=h, s=s, strided=True, name="filter_stage1")
    kspec = _filt2(a1f, f2r, f2i, twr, twi, kc=kc, ntb=2)

    skip4 = skip.reshape(2, NT, 1, LANES)

    a1 = _s1(f1_bf, p6, nat, lambda bb, t, j: (bb, 2, t, 0, j, 0), nb=b, nt=NT, rows=h, s=s, strided=True,
             name="conv_stage1")
    bsp = _conv2(a1, f2r, f2i, twr, twi, kspec, 0, kc=kc, ntb=ntb)
    zz1 = _s3(f3i_bf, bsp, p6, 0, p6, 2, skip4, 0, s=s, out_natural=False)
    a1 = _s1(f1_bf, zz1, (None, None, s, h, LANES), lambda bb, t, j: (bb, t, j, 0, 0), nb=b, nt=NT, rows=h, s=s,
             strided=False, name="conv_stage1")
    bsp = _conv2(a1, f2r, f2i, twr, twi, kspec, 1, kc=kc, ntb=ntb)
    zz2 = _s3(f3i_bf, bsp, p6, 1, zz1, None, skip4, 1, s=s, out_natural=True)

    z = _chdft(p, fc_bf, fnet_w_bf, tm=min(512, L))
    sf = min(N2, 4096 // (2 * n1f))
    a1z = _s1(fn1_bf, z.reshape(b, 2, NT, n1f, N2, LANES), (None, 2, None, n1f, sf, LANES),
              lambda bb, t, j: (bb, 0, t, 0, j, 0), nb=b, nt=NT, rows=2 * n1f, s=sf, strided=True,
              name="fnet_stage1")
    yg = _fnet2(a1z, f2r, f2i, ftwr, ftwi, fnet_b, kc=kc, ntb=ntb)

    return _outproj(zz2.reshape(b, NT, L, LANES), p, yg.reshape(b, NT, L, LANES), x, norm_hy, norm_fn,
                    w_out_bf, final_norm, tm=min(512, L))


def kernel(x_prompt, x_sample, norm_g, w_in, conv_w, conv_b, filt_w1, filt_b1, filt_w2, filt_b2,
           filt_w3, filt_b3, filt_w4, filt_freq, filt_skip, fnet_w, fnet_b, norm_hy, norm_fn,
           w_out, final_norm):
    l = 0
    w1p = jnp.zeros((FILTER_HIDDEN, FILTER_HIDDEN), F32).at[:FILTER_EMB].set(filt_w1[l])
    args = (norm_g[l], w_in[l].astype(BF16), conv_w[l], conv_b[l], w1p, filt_b1[l], filt_w2[l],
            filt_b2[l], filt_w3[l], filt_b3[l], filt_w4[l], filt_freq[l], filt_skip[l],
            fnet_w[l].astype(BF16), fnet_b[l], norm_hy[l], norm_fn[l], w_out[l].astype(BF16),
            final_norm)
    return (_trunk(x_prompt, *args), _trunk(x_sample, *args))
```

```python
import functools
import math

import numpy as np
import jax
import jax.numpy as jnp
from jax import lax
from jax.experimental import pallas as pl
from jax.experimental.pallas import tpu as pltpu

F32 = jnp.float32
BF16 = jnp.bfloat16

D_MODEL = 2048
D_HYENA = D_MODEL // 2
D_FNET = D_MODEL - D_HYENA
HYENA_HEADS = 8
HYENA_HEAD_DIM = D_HYENA // HYENA_HEADS
FNET_GROUPS = 4
FNET_GROUP_DIM = D_FNET // FNET_GROUPS
FILTER_EMB = 33
FILTER_HIDDEN = 64
MIN_DECAY = math.log(1e-2) / 1.5
MAX_DECAY = math.log(1e-2) / 0.3
EPS = 1e-6

LANES = 128
N_PROJ_GROUPS = 6
CH = 1024
NT = CH // LANES
N2 = 128
HALO = 16
VMEM_LIMIT = 52 * 1024 * 1024


def _cparams(*sem):
    return pltpu.CompilerParams(dimension_semantics=sem, vmem_limit_bytes=VMEM_LIMIT)


def _conv_tables(L):
    n = 2 * L
    n1 = n // N2
    h = n1 // 2
    k1 = np.arange(h, dtype=np.float64)
    ang1 = 2.0 * np.pi * np.outer(k1 + 0.5, np.arange(h)) / n1
    f1 = np.concatenate([np.cos(ang1), -np.sin(ang1)], axis=0)
    f3 = (2.0 / n) * np.stack([np.cos(ang1).T, -np.sin(ang1).T], axis=-1).reshape(h, 2 * h)
    angw = 2.0 * np.pi * np.outer(k1 + 0.5, np.arange(N2)) / n
    return (f1.astype(np.float32), f3.astype(np.float32),
            np.cos(angw).astype(np.float32), (-np.sin(angw)).astype(np.float32))


def _f2_tables():
    ang = 2.0 * np.pi * np.outer(np.arange(N2), np.arange(N2)) / N2
    return np.cos(ang).astype(np.float32), (-np.sin(ang)).astype(np.float32)


def _fnet_tables(L):
    n1 = L // N2
    ang1 = 2.0 * np.pi * np.outer(np.arange(n1), np.arange(n1)) / n1
    c, s = np.cos(ang1), np.sin(ang1)
    fn1 = np.block([[c, s], [-s, c]])
    angw = 2.0 * np.pi * np.outer(np.arange(n1), np.arange(N2)) / L
    scale = 1.0 / math.sqrt(L * FNET_GROUP_DIM)
    gd = FNET_GROUP_DIM
    angc = 2.0 * np.pi * np.outer(np.arange(gd), np.arange(gd)) / gd
    fc = np.concatenate([np.cos(angc), -np.sin(angc)], axis=1)
    return (fn1.astype(np.float32), (scale * np.cos(angw)).astype(np.float32),
            (-scale * np.sin(angw)).astype(np.float32), fc.astype(np.float32))


def _positional_features(L):
    bands = (FILTER_EMB - 1) // 2
    t = np.linspace(0.0, 1.0, L)
    w = 2.0 * np.pi * np.arange(L) / L
    f = np.linspace(1e-4, bands - 1, bands)
    fw = w[:, None] * f[None, :]
    z = np.concatenate([t[:, None], np.cos(fw), -np.sin(fw)], axis=-1)
    zp = np.zeros((L, FILTER_HIDDEN), np.float64)
    zp[:, :FILTER_EMB] = z
    return zp.astype(np.float32)


def _store_tiles(o_ref, val):
    for t in range(val.shape[-1] // LANES):
        o_ref[t] = val[:, t * LANES:(t + 1) * LANES].astype(o_ref.dtype)


def _inproj_kernel(xm_ref, xt_ref, xb_ref, g_ref, w_ref, cw_ref, cb_ref, o_ref, oz_ref, hn_ref, *, tm, n_conv):
    i = pl.program_id(1)
    j = pl.program_id(2)
    last_i = pl.num_programs(1) - 1

    @pl.when(j == 0)
    def _():
        g = g_ref[...]

        def nrm(x):
            ms = jnp.mean(x * x, axis=-1, keepdims=True)
            return x * lax.rsqrt(ms + EPS) * g

        hn_ref[HALO:HALO + tm, :] = nrm(xm_ref[...]).astype(BF16)
        top = jnp.where(i > 0, nrm(xt_ref[...]), 0.0)
        bot = jnp.where(i < last_i, nrm(xb_ref[...]), 0.0)
        hn_ref[0:HALO, :] = top.astype(BF16)
        hn_ref[HALO + tm:, :] = bot.astype(BF16)

    @pl.when(j < n_conv)
    def _():
        rows = tm + 2 * HALO
        p = jnp.dot(hn_ref[...], w_ref[...], preferred_element_type=F32)
        prev = pltpu.roll(p, 1, axis=0)
        nxt = pltpu.roll(p, rows - 1, axis=0)
        cw = cw_ref[...]
        u = prev * cw[0:1, :] + p * cw[1:2, :] + nxt * cw[2:3, :] + cb_ref[...]
        _store_tiles(o_ref, u[HALO:HALO + tm, :])

    @pl.when(j >= n_conv)
    def _():
        _store_tiles(oz_ref, jnp.dot(hn_ref[HALO:HALO + tm, :], w_ref[...], preferred_element_type=F32))


def _inproj(x, norm_g, w_in_bf, conv_w, conv_b, *, tm):
    b, L, d = x.shape
    n_conv = 3
    rt = tm // HALO
    nhalo = L // HALO
    kern = functools.partial(_inproj_kernel, tm=tm, n_conv=n_conv)
    return pl.pallas_call(
        kern,
        grid=(b, L // tm, N_PROJ_GROUPS),
        in_specs=[
            pl.BlockSpec((None, tm, d), lambda bb, i, j: (bb, i, 0)),
            pl.BlockSpec((None, HALO, d), lambda bb, i, j: (bb, jnp.maximum(i * rt - 1, 0), 0)),
            pl.BlockSpec((None, HALO, d), lambda bb, i, j: (bb, jnp.minimum((i + 1) * rt, nhalo - 1), 0)),
            pl.BlockSpec((1, d), lambda bb, i, j: (0, 0)),
            pl.BlockSpec((d, CH), lambda bb, i, j: (0, j)),
            pl.BlockSpec((3, CH), lambda bb, i, j: (0, jnp.minimum(j, n_conv - 1))),
            pl.BlockSpec((1, CH), lambda bb, i, j: (0, jnp.minimum(j, n_conv - 1))),
        ],
        out_specs=[
            pl.BlockSpec((None, None, NT, tm, LANES), lambda bb, i, j: (bb, jnp.minimum(j, n_conv - 1), 0, i, 0)),
            pl.BlockSpec((None, None, NT, tm, LANES), lambda bb, i, j: (bb, jnp.maximum(j - n_conv, 0), 0, i, 0)),
        ],
        out_shape=[
            jax.ShapeDtypeStruct((b, n_conv, NT, L, LANES), F32),
            jax.ShapeDtypeStruct((b, N_PROJ_GROUPS - n_conv, NT, L, LANES), BF16),
        ],
        scratch_shapes=[pltpu.VMEM((tm + 2 * HALO, d), BF16)],
        compiler_params=_cparams("arbitrary", "arbitrary", "arbitrary"),
        name="inproj",
    )(x, x, x, norm_g.reshape(1, d), w_in_bf, conv_w, conv_b.reshape(1, -1))


def _filter_kernel(z_ref, w1_ref, b1_ref, w2_ref, b2_ref, w3_ref, b3_ref, fr_ref, w4_ref, dl_ref,
                   o_ref, h_ref, *, tr, L):
    i = pl.program_id(0)
    j = pl.program_id(1)
    hp = lax.Precision.HIGHEST

    @pl.when(j == 0)
    def _():
        fr = fr_ref[...]
        h = jnp.sin(fr * (jnp.dot(z_ref[...], w1_ref[...], precision=hp, preferred_element_type=F32) + b1_ref[...]))
        h = jnp.sin(fr * (jnp.dot(h, w2_ref[...], precision=hp, preferred_element_type=F32) + b2_ref[...]))
        h = jnp.sin(fr * (jnp.dot(h, w3_ref[...], precision=hp, preferred_element_type=F32) + b3_ref[...]))
        hi = h.astype(BF16)
        lo = (h - hi.astype(F32)).astype(BF16)
        h_ref[...] = jnp.concatenate([hi, lo, hi], axis=1)

    f = jnp.dot(h_ref[...], w4_ref[...], preferred_element_type=F32)
    row = i * tr + lax.broadcasted_iota(jnp.int32, (tr, 1), 0)
    t = row.astype(F32) * (1.0 / (L - 1))
    f = f * jnp.exp(-t * jnp.abs(dl_ref[...]))
    drop = jnp.logical_and(row == 0, (j % 2) == 1)
    _store_tiles(o_ref, jnp.where(drop, 0.0, f))


def _filters(L, w1p, b1, w2, b2, w3, b3, freq, w4, *, tr):
    z = jnp.asarray(_positional_features(L))
    deltas = np.tile(np.linspace(MIN_DECAY, MAX_DECAY, HYENA_HEAD_DIM), HYENA_HEADS).astype(np.float32)
    fh = FILTER_HIDDEN
    ncol = w4.shape[1] // CH
    vec = lambda: pl.BlockSpec((1, fh), lambda i, j: (0, 0))
    mat = lambda: pl.BlockSpec((fh, fh), lambda i, j: (0, 0))
    kern = functools.partial(_filter_kernel, tr=tr, L=L)
    w4_hi = w4.astype(BF16)
    w4_lo = (w4 - w4_hi.astype(F32)).astype(BF16)
    w4 = jnp.concatenate([w4_hi, w4_hi, w4_lo], axis=0)
    return pl.pallas_call(
        kern,
        grid=(L // tr, ncol),
        in_specs=[
            pl.BlockSpec((tr, fh), lambda i, j: (i, 0)),
            mat(), vec(), mat(), vec(), mat(), vec(), vec(),
            pl.BlockSpec((3 * fh, CH), lambda i, j: (0, j)),
            pl.BlockSpec((1, CH), lambda i, j: (0, 0)),
        ],
        out_specs=pl.BlockSpec((NT, tr, LANES), lambda i, j: (j, i, 0)),
        out_shape=jax.ShapeDtypeStruct((ncol * NT, L, LANES), F32),
        scratch_shapes=[pltpu.VMEM((tr, 3 * fh), BF16)],
        compiler_params=_cparams("arbitrary", "arbitrary"),
        name="filter_mlp",
    )(z, w1p, b1.reshape(1, fh), w2, b2.reshape(1, fh), w3, b3.reshape(1, fh), freq.reshape(1, fh),
      w4, jnp.asarray(deltas).reshape(1, CH))


def _s1_kernel(m_ref, x_ref, o_ref, *, rows, s, strided, unroll):
    m = m_ref[...]
    x2 = x_ref.reshape(rows * s, LANES) if strided else None

    def body(j, carry):
        xj = x2[pl.ds(j, rows, stride=s), :] if strided else x_ref[j]
        o_ref[j] = jnp.dot(m, xj.astype(BF16), preferred_element_type=F32)
        return carry

    lax.fori_loop(0, s, body, 0, unroll=unroll)


def _s1(mat_bf, x, x_block, x_index, *, nb, nt, rows, s, strided, name):
    m, k = mat_bf.shape
    assert k == rows
    kern = functools.partial(_s1_kernel, rows=rows, s=s, strided=strided, unroll=8)
    return pl.pallas_call(
        kern,
        grid=(nb, nt, N2 // s),
        in_specs=[
            pl.BlockSpec((m, k), lambda bb, t, j: (0, 0)),
            pl.BlockSpec(x_block, x_index),
        ],
        out_specs=pl.BlockSpec((None, None, s, m, LANES), lambda bb, t, j: (bb, t, j, 0, 0)),
        out_shape=jax.ShapeDtypeStruct((nb, nt, N2, m, LANES), F32),
        compiler_params=_cparams("arbitrary", "arbitrary", "arbitrary"),
        name=name,
    )(mat_bf, x)


def _g_matrix(f2r, f2i, twr, twi):
    gr = f2r * twr - f2i * twi
    gi = f2r * twi + f2i * twr
    top = jnp.concatenate([gr, -gi], axis=1)
    bot = jnp.concatenate([gi, gr], axis=1)
    return jnp.concatenate([top, bot], axis=0).astype(BF16)


def _load_a(re_ref, im_ref, q, kc):
    def cat(ref):
        parts = [ref.at[t].reshape(N2 * kc, LANES)[pl.ds(q, N2, stride=kc), :] for t in range(ref.shape[0])]
        return parts[0] if len(parts) == 1 else jnp.concatenate(parts, axis=1)
    return jnp.concatenate([cat(re_ref), cat(im_ref)], axis=0).astype(BF16)


def _cat_tiles(ref, q):
    parts = [ref[t, q] for t in range(ref.shape[0])]
    return parts[0] if len(parts) == 1 else jnp.concatenate(parts, axis=1)


def _conv2_kernel(re_ref, im_ref, f2r_ref, f2i_ref, twr_ref, twi_ref, k_ref, o_ref, *, kc):
    f2r = f2r_ref[...]
    f2i = f2i_ref[...]
    for q in range(kc):
        g = _g_matrix(f2r, f2i, twr_ref[q:q + 1, :], twi_ref[q:q + 1, :])
        x = jnp.dot(g, _load_a(re_ref, im_ref, q, kc), preferred_element_type=F32)
        k = _cat_tiles(k_ref, q)
        xr, xi = x[:N2], x[N2:]
        kr, ki = k[:N2], k[N2:]
        y = jnp.concatenate([xr * kr - xi * ki, xr * ki + xi * kr], axis=0).astype(BF16)
        bk = lax.dot_general(g, y, (((0,), (0,)), ((), ())), preferred_element_type=F32)
        for t in range(o_ref.shape[0]):
            o_ref[t, q] = bk[:, t * LANES:(t + 1) * LANES]


def _conv2(a1, f2r, f2i, twr, twi, kspec, order, *, kc, ntb):
    b, nt, _, m, _ = a1.shape
    h = m // 2
    hb = h // kc
    kern = functools.partial(_conv2_kernel, kc=kc)
    return pl.pallas_call(
        kern,
        grid=(b, nt // ntb, hb),
        in_specs=[
            pl.BlockSpec((None, ntb, N2, kc, LANES), lambda bb, t, k: (bb, t, 0, k, 0)),
            pl.BlockSpec((None, ntb, N2, kc, LANES), lambda bb, t, k: (bb, t, 0, hb + k, 0)),
            pl.BlockSpec((N2, N2), lambda bb, t, k: (0, 0)),
            pl.BlockSpec((N2, N2), lambda bb, t, k: (0, 0)),
            pl.BlockSpec((kc, N2), lambda bb, t, k: (k, 0)),
            pl.BlockSpec((kc, N2), lambda bb, t, k: (k, 0)),
            pl.BlockSpec((None, ntb, kc, 2 * N2, LANES), lambda bb, t, k: (order, t, k, 0, 0)),
        ],
        out_specs=pl.BlockSpec((None, ntb, kc, 2 * N2, LANES), lambda bb, t, k: (bb, t, k, 0, 0)),
        out_shape=jax.ShapeDtypeStruct((b, nt, h, 2 * N2, LANES), F32),
        compiler_params=_cparams("arbitrary", "arbitrary", "arbitrary"),
        name="conv_stage2",
    )(a1, a1, f2r, f2i, twr, twi, kspec)


def _filt2_kernel(ref_ref, imf_ref, reb_ref, imb_ref, f2r_ref, f2i_ref, twr_ref, twi_ref, o_ref, *, kc):
    f2r = f2r_ref[...]
    f2i = f2i_ref[...]
    for q in range(kc):
        g = _g_matrix(f2r, f2i, twr_ref[q:q + 1, :], twi_ref[q:q + 1, :])
        xf = jnp.dot(g, _load_a(ref_ref, imf_ref, q, kc), preferred_element_type=F32)
        xb = jnp.dot(g, _load_a(reb_ref, imb_ref, q, kc), preferred_element_type=F32)
        k = jnp.concatenate([xf[:N2] + xb[:N2], xf[N2:] - xb[N2:]], axis=0)
        for t in range(o_ref.shape[0]):
            o_ref[t, q] = k[:, t * LANES:(t + 1) * LANES]


def _filt2(a1f, f2r, f2i, twr, twi, *, kc, ntb):
    _, _, _, m, _ = a1f.shape
    h = m // 2
    hb = h // kc
    ntg = NT // ntb
    kern = functools.partial(_filt2_kernel, kc=kc)
    blk = lambda im, d: pl.BlockSpec(
        (None, ntb, N2, kc, LANES), lambda o, t, k: (0, (2 * o + d) * ntg + t, 0, im * hb + k, 0))
    return pl.pallas_call(
        kern,
        grid=(2, ntg, hb),
        in_specs=[
            blk(0, 0), blk(1, 0), blk(0, 1), blk(1, 1),
            pl.BlockSpec((N2, N2), lambda o, t, k: (0, 0)),
            pl.BlockSpec((N2, N2), lambda o, t, k: (0, 0)),
            pl.BlockSpec((kc, N2), lambda o, t, k: (k, 0)),
            pl.BlockSpec((kc, N2), lambda o, t, k: (k, 0)),
        ],
        out_specs=pl.BlockSpec((None, ntb, kc, 2 * N2, LANES), lambda o, t, k: (o, t, k, 0, 0)),
        out_shape=jax.ShapeDtypeStruct((2, NT, h, 2 * N2, LANES), F32),
        compiler_params=_cparams("arbitrary", "arbitrary", "arbitrary"),
        name="filter_stage2",
    )(a1f, a1f, a1f, a1f, f2r, f2i, twr, twi)


def _s3_kernel(m_ref, b_ref, gate_ref, v_ref, skip_ref, o_ref, *, h, s, v_strided, out_strided, unroll):
    m = m_ref[...]
    skip = skip_ref[...]
    b2 = b_ref.reshape(h * 2 * s, LANES)
    gate2 = gate_ref.reshape(h * s, LANES)
    v2 = v_ref.reshape(h * s, LANES) if v_strided else None

    def body(j, carry):
        bj = b2[pl.ds(j, 2 * h, stride=s), :].astype(BF16)
        y = jnp.dot(m, bj, preferred_element_type=F32)
        vj = v2[pl.ds(j, h, stride=s), :] if v_strided else v_ref[j]
        zz = gate2[pl.ds(j, h, stride=s), :] * (y + skip * vj)
        if out_strided:
            o_ref[:, pl.ds(j, 1), :] = zz.reshape(h, 1, LANES)
        else:
            o_ref[j] = zz
        return carry

    lax.fori_loop(0, s, body, 0, unroll=unroll)


def _s3(f3i_bf, bspec, gate6, gate_group, v, v_group, skip4, order, *, s, out_natural):
    b, nt, h, _, _ = bspec.shape
    b6 = bspec.reshape(b, nt, h, 2, N2, LANES)
    nat = (None, None, None, h, s, LANES)
    v_strided = v_group is not None
    if v_strided:
        v_spec = pl.BlockSpec(nat, lambda bb, t, j: (bb, v_group, t, 0, j, 0))
    else:
        v_spec = pl.BlockSpec((None, None, s, h, LANES), lambda bb, t, j: (bb, t, j, 0, 0))
    if out_natural:
        out_spec = pl.BlockSpec((None, None, h, s, LANES), lambda bb, t, j: (bb, t, 0, j, 0))
        out_shape = jax.ShapeDtypeStruct((b, nt, h, N2, LANES), F32)
    else:
        out_spec = pl.BlockSpec((None, None, s, h, LANES), lambda bb, t, j: (bb, t, j, 0, 0))
        out_shape = jax.ShapeDtypeStruct((b, nt, N2, h, LANES), F32)
    kern = functools.partial(_s3_kernel, h=h, s=s, v_strided=v_strided, out_strided=out_natural, unroll=4)
    return pl.pallas_call(
        kern,
        grid=(b, nt, N2 // s),
        in_specs=[
            pl.BlockSpec((h, 2 * h), lambda bb, t, j: (0, 0)),
            pl.BlockSpec((None, None, h, 2, s, LANES), lambda bb, t, j: (bb, t, 0, 0, j, 0)),
            pl.BlockSpec(nat, lambda bb, t, j: (bb, gate_group, t, 0, j, 0)),
            v_spec,
            pl.BlockSpec((None, None, 1, LANES), lambda bb, t, j: (order, t, 0, 0)),
        ],
        out_specs=out_spec,
        out_shape=out_shape,
        compiler_params=_cparams("arbitrary", "arbitrary", "arbitrary"),
        name="conv_stage3",
    )(f3i_bf, b6, gate6, v, skip4)


def _chdft_kernel(u_ref, fc_ref, w_ref, o_ref):
    gd = FNET_GROUP_DIM
    tpg = gd // LANES
    fc = fc_ref[...]
    for g in range(FNET_GROUPS):
        u = jnp.concatenate([u_ref[g * tpg + t] for t in range(tpg)], axis=1).astype(BF16)
        z = jnp.dot(u, fc, preferred_element_type=F32)
        w = w_ref[g]
        zr = jnp.dot(z[:, :gd].astype(BF16), w, preferred_element_type=F32)
        zi = jnp.dot(z[:, gd:].astype(BF16), w, preferred_element_type=F32)
        for t in range(tpg):
            o_ref[0, g * tpg + t] = zr[:, t * LANES:(t + 1) * LANES]
            o_ref[1, g * tpg + t] = zi[:, t * LANES:(t + 1) * LANES]


def _chdft(p, fc_bf, fnet_w_bf, *, tm):
    b, _, nt, L, _ = p.shape
    gd = FNET_GROUP_DIM
    return pl.pallas_call(
        _chdft_kernel,
        grid=(b, L // tm),
        in_specs=[
            pl.BlockSpec((None, None, nt, tm, LANES), lambda bb, i: (bb, 1, 0, i, 0)),
            pl.BlockSpec((gd, 2 * gd), lambda bb, i: (0, 0)),
            pl.BlockSpec((FNET_GROUPS, gd, gd), lambda bb, i: (0, 0, 0)),
        ],
        out_specs=pl.BlockSpec((None, 2, nt, tm, LANES), lambda bb, i: (bb, 0, 0, i, 0)),
        out_shape=jax.ShapeDtypeStruct((b, 2, nt, L, LANES), F32),
        compiler_params=_cparams("arbitrary", "arbitrary"),
        name="fnet_chdft",
    )(p, fc_bf, fnet_w_bf)


def _fnet2_kernel(re_ref, im_ref, f2r_ref, f2i_ref, twr_ref, twi_ref, b_ref, o_ref, *, kc):
    f2r = f2r_ref[...]
    f2i = f2i_ref[...]
    for q in range(kc):
        twr = twr_ref[q:q + 1, :]
        twi = twi_ref[q:q + 1, :]
        g = jnp.concatenate([f2r * twr - f2i * twi, -(f2r * twi + f2i * twr)], axis=1).astype(BF16)
        fg = jnp.dot(g, _load_a(re_ref, im_ref, q, kc), preferred_element_type=F32)
        for t in range(o_ref.shape[0]):
            o_ref[t, :, q:q + 1, :] = (fg[:, t * LANES:(t + 1) * LANES] + b_ref[t]).reshape(N2, 1, LANES)


def _fnet2(a1, f2r, f2i, twr, twi, fnet_b, *, kc, ntb):
    b, nt, _, m, _ = a1.shape
    n1 = m // 2
    nb = n1 // kc
    kern = functools.partial(_fnet2_kernel, kc=kc)
    return pl.pallas_call(
        kern,
        grid=(b, nt // ntb, nb),
        in_specs=[
            pl.BlockSpec((None, ntb, N2, kc, LANES), lambda bb, t, k: (bb, t, 0, k, 0)),
            pl.BlockSpec((None, ntb, N2, kc, LANES), lambda bb, t, k: (bb, t, 0, nb + k, 0)),
            pl.BlockSpec((N2, N2), lambda bb, t, k: (0, 0)),
            pl.BlockSpec((N2, N2), lambda bb, t, k: (0, 0)),
            pl.BlockSpec((kc, N2), lambda bb, t, k: (k, 0)),
            pl.BlockSpec((kc, N2), lambda bb, t, k: (k, 0)),
            pl.BlockSpec((ntb, 1, LANES), lambda bb, t, k: (t, 0, 0)),
        ],
        out_specs=pl.BlockSpec((None, ntb, N2, kc, LANES), lambda bb, t, k: (bb, t, 0, k, 0)),
        out_shape=jax.ShapeDtypeStruct((b, nt, N2, n1, LANES), F32),
        compiler_params=_cparams("arbitrary", "arbitrary", "arbitrary"),
        name="fnet_stage2",
    )(a1, a1, f2r, f2i, twr, twi, fnet_b.reshape(nt, 1, LANES))


def _outproj_kernel(zz_ref, zhy_ref, yg_ref, zfn_ref, x_ref, nhy_ref, nfn_ref, w_ref, fin_ref,
                    o_ref, cat_ref):
    def silu(z):
        return z / (1.0 + jnp.exp(-z))

    def branch(a_ref, z_ref, g_ref, col0):
        nt = a_ref.shape[0]
        a = [a_ref[t] * silu(z_ref[t].astype(F32)) for t in range(nt)]
        ss = sum(jnp.sum(at * at, axis=-1, keepdims=True) for at in a)
        rs = lax.rsqrt(ss * (1.0 / (nt * LANES)) + EPS)
        for t in range(nt):
            lo = col0 + t * LANES
            cat_ref[:, lo:lo + LANES] = (a[t] * rs * g_ref[:, t * LANES:(t + 1) * LANES]).astype(BF16)

    branch(zz_ref, zhy_ref, nhy_ref, 0)
    branch(yg_ref, zfn_ref, nfn_ref, zz_ref.shape[0] * LANES)
    y = x_ref[...] + jnp.dot(cat_ref[...], w_ref[...], preferred_element_type=F32)
    ms = jnp.mean(y * y, axis=-1, keepdims=True)
    o_ref[...] = y * lax.rsqrt(ms + EPS) * fin_ref[...]


def _outproj(zz, p, yg, x, norm_hy, norm_fn, w_out_bf, final_norm, *, tm):
    b, L, d = x.shape
    c = CH
    vecc = lambda: pl.BlockSpec((1, c), lambda bb, i: (0, 0))
    tiles = lambda: pl.BlockSpec((None, NT, tm, LANES), lambda bb, i: (bb, 0, i, 0))
    group = lambda g: pl.BlockSpec((None, None, NT, tm, LANES), lambda bb, i: (bb, g, 0, i, 0))
    return pl.pallas_call(
        _outproj_kernel,
        grid=(b, L // tm),
        in_specs=[
            tiles(), group(0), tiles(), group(2),
            pl.BlockSpec((None, tm, d), lambda bb, i: (bb, i, 0)),
            vecc(), vecc(),
            pl.BlockSpec((2 * c, d), lambda bb, i: (0, 0)),
            pl.BlockSpec((1, d), lambda bb, i: (0, 0)),
        ],
        out_specs=pl.BlockSpec((None, tm, d), lambda bb, i: (bb, i, 0)),
        out_shape=jax.ShapeDtypeStruct((b, L, d), F32),
        scratch_shapes=[pltpu.VMEM((tm, 2 * c), BF16)],
        compiler_params=_cparams("arbitrary", "arbitrary"),
        name="outproj",
    )(zz, p, yg, p, x, norm_hy.reshape(1, c), norm_fn.reshape(1, c), w_out_bf, final_norm.reshape(1, d))


def _trunk(x, norm_g, w_in_bf, conv_w, conv_b, w1p, b1, w2, b2, w3, b3, w4, freq, skip,
           fnet_w_bf, fnet_b, norm_hy, norm_fn, w_out_bf, final_norm):
    b, L, d = x.shape
    n = 2 * L
    h = n // N2 // 2
    n1f = L // N2
    s = min(N2, 4096 // h)
    kc = 8
    ntb = 4

    f1, f3, twr, twi = (jnp.asarray(t) for t in _conv_tables(L))
    f2r, f2i = (jnp.asarray(t) for t in _f2_tables())
    fn1, ftwr, ftwi, fc = (jnp.asarray(t) for t in _fnet_tables(L))
    f1_bf, f3i_bf, fn1_bf, fc_bf = (t.astype(BF16) for t in (f1, f3, fn1, fc))

    p, pz = _inproj(x, norm_g, w_in_bf, conv_w, conv_b, tm=min(1024, L))
    p6 = p.reshape(b, 3, NT, h, N2, LANES)
    nat = (None, None, None, h, s, LANES)

    filt = _filters(L, w1p, b1, w2, b2, w3, b3, freq, w4, tr=min(512, L))
    a1f = _s1(f1_bf, filt.reshape(1, 1, 4 * NT, h, N2, LANES), nat, lambda bb, t, j: (0, 0, t, 0, j, 0),
              nb=1, nt=4 * NT, rows=h, s=s, strided=True, name="filter_stage1")
    kspec = _filt2(a1f, f2r, f2i, twr, twi, kc=kc, ntb=2)

    skip4 = skip.reshape(2, NT, 1, LANES)

    a1 = _s1(f1_bf, p6, nat, lambda bb, t, j: (bb, 2, t, 0, j, 0), nb=b, nt=NT, rows=h, s=s, strided=True,
             name="conv_stage1")
    bsp = _conv2(a1, f2r, f2i, twr, twi, kspec, 0, kc=kc, ntb=ntb)
    zz1 = _s3(f3i_bf, bsp, p6, 0, p6, 2, skip4, 0, s=s, out_natural=False)
    a1 = _s1(f1_bf, zz1, (None, None, s, h, LANES), lambda bb, t, j: (bb, t, j, 0, 0), nb=b, nt=NT, rows=h, s=s,
             strided=False, name="conv_stage1")
    bsp = _conv2(a1, f2r, f2i, twr, twi, kspec, 1, kc=kc, ntb=ntb)
    zz2 = _s3(f3i_bf, bsp, p6, 1, zz1, None, skip4, 1, s=s, out_natural=True)

    z = _chdft(pz, fc_bf, fnet_w_bf, tm=min(512, L))
    sf = min(N2, 4096 // (2 * n1f))
    a1z = _s1(fn1_bf, z.reshape(b, 2, NT, n1f, N2, LANES), (None, 2, None, n1f, sf, LANES),
              lambda bb, t, j: (bb, 0, t, 0, j, 0), nb=b, nt=NT, rows=2 * n1f, s=sf, strided=True,
              name="fnet_stage1")
    yg = _fnet2(a1z, f2r, f2i, ftwr, ftwi, fnet_b, kc=kc, ntb=ntb)

    return _outproj(zz2.reshape(b, NT, L, LANES), pz, yg.reshape(b, NT, L, LANES), x, norm_hy, norm_fn,
                    w_out_bf, final_norm, tm=min(512, L))


def kernel(x_prompt, x_sample, norm_g, w_in, conv_w, conv_b, filt_w1, filt_b1, filt_w2, filt_b2,
           filt_w3, filt_b3, filt_w4, filt_freq, filt_skip, fnet_w, fnet_b, norm_hy, norm_fn,
           w_out, final_norm):
    l = 0
    w1p = jnp.zeros((FILTER_HIDDEN, FILTER_HIDDEN), F32).at[:FILTER_EMB].set(filt_w1[l])
    args = (norm_g[l], w_in[l].astype(BF16), conv_w[l], conv_b[l], w1p, filt_b1[l], filt_w2[l],
            filt_b2[l], filt_w3[l], filt_b3[l], filt_w4[l], filt_freq[l], filt_skip[l],
            fnet_w[l].astype(BF16), fnet_b[l], norm_hy[l], norm_fn[l], w_out[l].astype(BF16),
            final_norm)
    return (_trunk(x_prompt, *args), _trunk(x_sample, *args))
```

```python
import functools
import math

import numpy as np
import jax
import jax.numpy as jnp
from jax import lax
from jax.experimental import pallas as pl
from jax.experimental.pallas import tpu as pltpu

F32 = jnp.float32
BF16 = jnp.bfloat16

D_MODEL = 2048
D_HYENA = D_MODEL // 2
D_FNET = D_MODEL - D_HYENA
HYENA_HEADS = 8
HYENA_HEAD_DIM = D_HYENA // HYENA_HEADS
FNET_GROUPS = 4
FNET_GROUP_DIM = D_FNET // FNET_GROUPS
FILTER_EMB = 33
FILTER_HIDDEN = 64
MIN_DECAY = math.log(1e-2) / 1.5
MAX_DECAY = math.log(1e-2) / 0.3
EPS = 1e-6

LANES = 128
N_PROJ_GROUPS = 6
CH = 1024
NT = CH // LANES
N2 = 128
HALO = 16
VMEM_LIMIT = 52 * 1024 * 1024


def _cparams(*sem):
    return pltpu.CompilerParams(dimension_semantics=sem, vmem_limit_bytes=VMEM_LIMIT)


def _conv_tables(L):
    n = 2 * L
    n1 = n // N2
    h = n1 // 2
    k1 = np.arange(h, dtype=np.float64)
    ang1 = 2.0 * np.pi * np.outer(k1 + 0.5, np.arange(h)) / n1
    f1 = np.concatenate([np.cos(ang1), -np.sin(ang1)], axis=0)
    f3 = (2.0 / n) * np.stack([np.cos(ang1).T, -np.sin(ang1).T], axis=-1).reshape(h, 2 * h)
    angw = 2.0 * np.pi * np.outer(k1 + 0.5, np.arange(N2)) / n
    return (f1.astype(np.float32), f3.astype(np.float32),
            np.cos(angw).astype(np.float32), (-np.sin(angw)).astype(np.float32))


def _f2_tables():
    ang = 2.0 * np.pi * np.outer(np.arange(N2), np.arange(N2)) / N2
    return np.cos(ang).astype(np.float32), (-np.sin(ang)).astype(np.float32)


def _fnet_tables(L):
    n1 = L // N2
    ang1 = 2.0 * np.pi * np.outer(np.arange(n1), np.arange(n1)) / n1
    c, s = np.cos(ang1), np.sin(ang1)
    fn1 = np.block([[c, s], [-s, c]])
    angw = 2.0 * np.pi * np.outer(np.arange(n1), np.arange(N2)) / L
    scale = 1.0 / math.sqrt(L * FNET_GROUP_DIM)
    gd = FNET_GROUP_DIM
    angc = 2.0 * np.pi * np.outer(np.arange(gd), np.arange(gd)) / gd
    fc = np.concatenate([np.cos(angc), -np.sin(angc)], axis=1)
    return (fn1.astype(np.float32), (scale * np.cos(angw)).astype(np.float32),
            (-scale * np.sin(angw)).astype(np.float32), fc.astype(np.float32))


def _positional_features(L):
    bands = (FILTER_EMB - 1) // 2
    t = np.linspace(0.0, 1.0, L)
    w = 2.0 * np.pi * np.arange(L) / L
    f = np.linspace(1e-4, bands - 1, bands)
    fw = w[:, None] * f[None, :]
    z = np.concatenate([t[:, None], np.cos(fw), -np.sin(fw)], axis=-1)
    zp = np.zeros((L, FILTER_HIDDEN), np.float64)
    zp[:, :FILTER_EMB] = z
    return zp.astype(np.float32)


def _store_tiles(o_ref, val):
    for t in range(val.shape[-1] // LANES):
        o_ref[t] = val[:, t * LANES:(t + 1) * LANES].astype(o_ref.dtype)


def _inproj_kernel(xm_ref, xt_ref, xb_ref, g_ref, w_ref, cw_ref, cb_ref, o_ref, oz_ref, hn_ref, *, tm, n_conv):
    i = pl.program_id(1)
    j = pl.program_id(2)
    last_i = pl.num_programs(1) - 1

    @pl.when(j == 0)
    def _():
        g = g_ref[...]

        def nrm(x):
            ms = jnp.mean(x * x, axis=-1, keepdims=True)
            return x * lax.rsqrt(ms + EPS) * g

        hn_ref[HALO:HALO + tm, :] = nrm(xm_ref[...]).astype(BF16)
        top = jnp.where(i > 0, nrm(xt_ref[...]), 0.0)
        bot = jnp.where(i < last_i, nrm(xb_ref[...]), 0.0)
        hn_ref[0:HALO, :] = top.astype(BF16)
        hn_ref[HALO + tm:, :] = bot.astype(BF16)

    @pl.when(j < n_conv)
    def _():
        rows = tm + 2 * HALO
        p = jnp.dot(hn_ref[...], w_ref[...], preferred_element_type=F32)
        prev = pltpu.roll(p, 1, axis=0)
        nxt = pltpu.roll(p, rows - 1, axis=0)
        cw = cw_ref[...]
        u = prev * cw[0:1, :] + p * cw[1:2, :] + nxt * cw[2:3, :] + cb_ref[...]
        _store_tiles(o_ref, u[HALO:HALO + tm, :])

    @pl.when(j >= n_conv)
    def _():
        _store_tiles(oz_ref, jnp.dot(hn_ref[HALO:HALO + tm, :], w_ref[...], preferred_element_type=F32))


def _inproj(x, norm_g, w_in_bf, conv_w, conv_b, *, tm):
    b, L, d = x.shape
    n_conv = 3
    rt = tm // HALO
    nhalo = L // HALO
    kern = functools.partial(_inproj_kernel, tm=tm, n_conv=n_conv)
    return pl.pallas_call(
        kern,
        grid=(b, L // tm, N_PROJ_GROUPS),
        in_specs=[
            pl.BlockSpec((None, tm, d), lambda bb, i, j: (bb, i, 0)),
            pl.BlockSpec((None, HALO, d), lambda bb, i, j: (bb, jnp.maximum(i * rt - 1, 0), 0)),
            pl.BlockSpec((None, HALO, d), lambda bb, i, j: (bb, jnp.minimum((i + 1) * rt, nhalo - 1), 0)),
            pl.BlockSpec((1, d), lambda bb, i, j: (0, 0)),
            pl.BlockSpec((d, CH), lambda bb, i, j: (0, j)),
            pl.BlockSpec((3, CH), lambda bb, i, j: (0, jnp.minimum(j, n_conv - 1))),
            pl.BlockSpec((1, CH), lambda bb, i, j: (0, jnp.minimum(j, n_conv - 1))),
        ],
        out_specs=[
            pl.BlockSpec((None, None, NT, tm, LANES), lambda bb, i, j: (bb, jnp.minimum(j, n_conv - 1), 0, i, 0)),
            pl.BlockSpec((None, None, NT, tm, LANES), lambda bb, i, j: (bb, jnp.maximum(j - n_conv, 0), 0, i, 0)),
        ],
        out_shape=[
            jax.ShapeDtypeStruct((b, n_conv, NT, L, LANES), F32),
            jax.ShapeDtypeStruct((b, N_PROJ_GROUPS - n_conv, NT, L, LANES), BF16),
        ],
        scratch_shapes=[pltpu.VMEM((tm + 2 * HALO, d), BF16)],
        compiler_params=_cparams("arbitrary", "arbitrary", "arbitrary"),
        name="inproj",
    )(x, x, x, norm_g.reshape(1, d), w_in_bf, conv_w, conv_b.reshape(1, -1))


def _filter_kernel(z_ref, w1_ref, b1_ref, w2_ref, b2_ref, w3_ref, b3_ref, fr_ref, w4_ref, dl_ref,
                   o_ref, h_ref, *, tr, L):
    i = pl.program_id(0)
    j = pl.program_id(1)
    hp = lax.Precision.HIGHEST

    @pl.when(j == 0)
    def _():
        fr = fr_ref[...]
        h = jnp.sin(fr * (jnp.dot(z_ref[...], w1_ref[...], precision=hp, preferred_element_type=F32) + b1_ref[...]))
        h = jnp.sin(fr * (jnp.dot(h, w2_ref[...], precision=hp, preferred_element_type=F32) + b2_ref[...]))
        h = jnp.sin(fr * (jnp.dot(h, w3_ref[...], precision=hp, preferred_element_type=F32) + b3_ref[...]))
        hi = h.astype(BF16)
        lo = (h - hi.astype(F32)).astype(BF16)
        h_ref[...] = jnp.concatenate([hi, lo, hi], axis=1)

    f = jnp.dot(h_ref[...], w4_ref[...], preferred_element_type=F32)
    row = i * tr + lax.broadcasted_iota(jnp.int32, (tr, 1), 0)
    t = row.astype(F32) * (1.0 / (L - 1))
    f = f * jnp.exp(-t * jnp.abs(dl_ref[...]))
    drop = jnp.logical_and(row == 0, (j % 2) == 1)
    _store_tiles(o_ref, jnp.where(drop, 0.0, f))


def _filters(L, w1p, b1, w2, b2, w3, b3, freq, w4, *, tr):
    z = jnp.asarray(_positional_features(L))
    deltas = np.tile(np.linspace(MIN_DECAY, MAX_DECAY, HYENA_HEAD_DIM), HYENA_HEADS).astype(np.float32)
    fh = FILTER_HIDDEN
    ncol = w4.shape[1] // CH
    vec = lambda: pl.BlockSpec((1, fh), lambda i, j: (0, 0))
    mat = lambda: pl.BlockSpec((fh, fh), lambda i, j: (0, 0))
    kern = functools.partial(_filter_kernel, tr=tr, L=L)
    w4_hi = w4.astype(BF16)
    w4_lo = (w4 - w4_hi.astype(F32)).astype(BF16)
    w4 = jnp.concatenate([w4_hi, w4_hi, w4_lo], axis=0)
    return pl.pallas_call(
        kern,
        grid=(L // tr, ncol),
        in_specs=[
            pl.BlockSpec((tr, fh), lambda i, j: (i, 0)),
            mat(), vec(), mat(), vec(), mat(), vec(), vec(),
            pl.BlockSpec((3 * fh, CH), lambda i, j: (0, j)),
            pl.BlockSpec((1, CH), lambda i, j: (0, 0)),
        ],
        out_specs=pl.BlockSpec((NT, tr, LANES), lambda i, j: (j, i, 0)),
        out_shape=jax.ShapeDtypeStruct((ncol * NT, L, LANES), F32),
        scratch_shapes=[pltpu.VMEM((tr, 3 * fh), BF16)],
        compiler_params=_cparams("arbitrary", "arbitrary"),
        name="filter_mlp",
    )(z, w1p, b1.reshape(1, fh), w2, b2.reshape(1, fh), w3, b3.reshape(1, fh), freq.reshape(1, fh),
      w4, jnp.asarray(deltas).reshape(1, CH))


def _s1_kernel(m_ref, x_ref, o_ref, *, rows, s, strided, unroll):
    m = m_ref[...]
    x2 = x_ref.reshape(rows * s, LANES) if strided else None

    def body(j, carry):
        xj = x2[pl.ds(j, rows, stride=s), :] if strided else x_ref[j]
        o_ref[j] = jnp.dot(m, xj.astype(BF16), preferred_element_type=F32)
        return carry

    lax.fori_loop(0, s, body, 0, unroll=unroll)


def _s1(mat_bf, x, x_block, x_index, *, nb, nt, rows, s, strided, name):
    m, k = mat_bf.shape
    assert k == rows
    kern = functools.partial(_s1_kernel, rows=rows, s=s, strided=strided, unroll=8)
    return pl.pallas_call(
        kern,
        grid=(nb, nt, N2 // s),
        in_specs=[
            pl.BlockSpec((m, k), lambda bb, t, j: (0, 0)),
            pl.BlockSpec(x_block, x_index),
        ],
        out_specs=pl.BlockSpec((None, None, s, m, LANES), lambda bb, t, j: (bb, t, j, 0, 0)),
        out_shape=jax.ShapeDtypeStruct((nb, nt, N2, m, LANES), F32),
        compiler_params=_cparams("arbitrary", "arbitrary", "arbitrary"),
        name=name,
    )(mat_bf, x)


def _g_matrix(f2r, f2i, twr, twi):
    gr = f2r * twr - f2i * twi
    gi = f2r * twi + f2i * twr
    top = jnp.concatenate([gr, -gi], axis=1)
    bot = jnp.concatenate([gi, gr], axis=1)
    return jnp.concatenate([top, bot], axis=0).astype(BF16)


def _load_a(re_ref, im_ref, q, kc):
    def cat(ref):
        parts = [ref.at[t].reshape(N2 * kc, LANES)[pl.ds(q, N2, stride=kc), :] for t in range(ref.shape[0])]
        return parts[0] if len(parts) == 1 else jnp.concatenate(parts, axis=1)
    return jnp.concatenate([cat(re_ref), cat(im_ref)], axis=0).astype(BF16)


def _cat_tiles(ref, q):
    parts = [ref[t, q] for t in range(ref.shape[0])]
    return parts[0] if len(parts) == 1 else jnp.concatenate(parts, axis=1)


def _conv2_kernel(re_ref, im_ref, f2r_ref, f2i_ref, twr_ref, twi_ref, k_ref, o_ref, *, kc):
    f2r = f2r_ref[...]
    f2i = f2i_ref[...]
    for q in range(kc):
        g = _g_matrix(f2r, f2i, twr_ref[q:q + 1, :], twi_ref[q:q + 1, :])
        x = jnp.dot(g, _load_a(re_ref, im_ref, q, kc), preferred_element_type=F32)
        k = _cat_tiles(k_ref, q)
        xr, xi = x[:N2], x[N2:]
        kr, ki = k[:N2], k[N2:]
        y = jnp.concatenate([xr * kr - xi * ki, xr * ki + xi * kr], axis=0).astype(BF16)
        bk = lax.dot_general(g, y, (((0,), (0,)), ((), ())), preferred_element_type=F32)
        for t in range(o_ref.shape[0]):
            o_ref[t, q] = bk[:, t * LANES:(t + 1) * LANES]


def _conv2(a1, f2r, f2i, twr, twi, kspec, order, *, kc, ntb):
    b, nt, _, m, _ = a1.shape
    h = m // 2
    hb = h // kc
    kern = functools.partial(_conv2_kernel, kc=kc)
    return pl.pallas_call(
        kern,
        grid=(b, nt // ntb, hb),
        in_specs=[
            pl.BlockSpec((None, ntb, N2, kc, LANES), lambda bb, t, k: (bb, t, 0, k, 0)),
            pl.BlockSpec((None, ntb, N2, kc, LANES), lambda bb, t, k: (bb, t, 0, hb + k, 0)),
            pl.BlockSpec((N2, N2), lambda bb, t, k: (0, 0)),
            pl.BlockSpec((N2, N2), lambda bb, t, k: (0, 0)),
            pl.BlockSpec((kc, N2), lambda bb, t, k: (k, 0)),
            pl.BlockSpec((kc, N2), lambda bb, t, k: (k, 0)),
            pl.BlockSpec((None, ntb, kc, 2 * N2, LANES), lambda bb, t, k: (order, t, k, 0, 0)),
        ],
        out_specs=pl.BlockSpec((None, ntb, kc, 2 * N2, LANES), lambda bb, t, k: (bb, t, k, 0, 0)),
        out_shape=jax.ShapeDtypeStruct((b, nt, h, 2 * N2, LANES), F32),
        compiler_params=_cparams("arbitrary", "arbitrary", "arbitrary"),
        name="conv_stage2",
    )(a1, a1, f2r, f2i, twr, twi, kspec)


def _filt2_kernel(ref_ref, imf_ref, reb_ref, imb_ref, f2r_ref, f2i_ref, twr_ref, twi_ref, o_ref, *, kc):
    f2r = f2r_ref[...]
    f2i = f2i_ref[...]
    for q in range(kc):
        g = _g_matrix(f2r, f2i, twr_ref[q:q + 1, :], twi_ref[q:q + 1, :])
        xf = jnp.dot(g, _load_a(ref_ref, imf_ref, q, kc), preferred_element_type=F32)
        xb = jnp.dot(g, _load_a(reb_ref, imb_ref, q, kc), preferred_element_type=F32)
        k = jnp.concatenate([xf[:N2] + xb[:N2], xf[N2:] - xb[N2:]], axis=0)
        for t in range(o_ref.shape[0]):
            o_ref[t, q] = k[:, t * LANES:(t + 1) * LANES]


def _filt2(a1f, f2r, f2i, twr, twi, *, kc, ntb):
    _, _, _, m, _ = a1f.shape
    h = m // 2
    hb = h // kc
    ntg = NT // ntb
    kern = functools.partial(_filt2_kernel, kc=kc)
    blk = lambda im, d: pl.BlockSpec(
        (None, ntb, N2, kc, LANES), lambda o, t, k: (0, (2 * o + d) * ntg + t, 0, im * hb + k, 0))
    return pl.pallas_call(
        kern,
        grid=(2, ntg, hb),
        in_specs=[
            blk(0, 0), blk(1, 0), blk(0, 1), blk(1, 1),
            pl.BlockSpec((N2, N2), lambda o, t, k: (0, 0)),
            pl.BlockSpec((N2, N2), lambda o, t, k: (0, 0)),
            pl.BlockSpec((kc, N2), lambda o, t, k: (k, 0)),
            pl.BlockSpec((kc, N2), lambda o, t, k: (k, 0)),
        ],
        out_specs=pl.BlockSpec((None, ntb, kc, 2 * N2, LANES), lambda o, t, k: (o, t, k, 0, 0)),
        out_shape=jax.ShapeDtypeStruct((2, NT, h, 2 * N2, LANES), F32),
        compiler_params=_cparams("arbitrary", "arbitrary", "arbitrary"),
        name="filter_stage2",
    )(a1f, a1f, a1f, a1f, f2r, f2i, twr, twi)


def _s3_kernel(m_ref, b_ref, gate_ref, v_ref, skip_ref, o_ref, *, h, s, v_strided, out_strided, unroll):
    m = m_ref[...]
    skip = skip_ref[...]
    b2 = b_ref.reshape(h * 2 * s, LANES)
    gate2 = gate_ref.reshape(h * s, LANES)
    v2 = v_ref.reshape(h * s, LANES) if v_strided else None

    def body(j, carry):
        bj = b2[pl.ds(j, 2 * h, stride=s), :].astype(BF16)
        y = jnp.dot(m, bj, preferred_element_type=F32)
        vj = v2[pl.ds(j, h, stride=s), :] if v_strided else v_ref[j]
        zz = gate2[pl.ds(j, h, stride=s), :] * (y + skip * vj)
        if out_strided:
            o_ref[:, pl.ds(j, 1), :] = zz.reshape(h, 1, LANES)
        else:
            o_ref[j] = zz
        return carry

    lax.fori_loop(0, s, body, 0, unroll=unroll)


def _s3(f3i_bf, bspec, gate6, gate_group, v, v_group, skip4, order, *, s, out_natural):
    b, nt, h, _, _ = bspec.shape
    b6 = bspec.reshape(b, nt, h, 2, N2, LANES)
    nat = (None, None, None, h, s, LANES)
    v_strided = v_group is not None
    if v_strided:
        v_spec = pl.BlockSpec(nat, lambda bb, t, j: (bb, v_group, t, 0, j, 0))
    else:
        v_spec = pl.BlockSpec((None, None, s, h, LANES), lambda bb, t, j: (bb, t, j, 0, 0))
    if out_natural:
        out_spec = pl.BlockSpec((None, None, h, s, LANES), lambda bb, t, j: (bb, t, 0, j, 0))
        out_shape = jax.ShapeDtypeStruct((b, nt, h, N2, LANES), F32)
    else:
        out_spec = pl.BlockSpec((None, None, s, h, LANES), lambda bb, t, j: (bb, t, j, 0, 0))
        out_shape = jax.ShapeDtypeStruct((b, nt, N2, h, LANES), F32)
    kern = functools.partial(_s3_kernel, h=h, s=s, v_strided=v_strided, out_strided=out_natural, unroll=4)
    return pl.pallas_call(
        kern,
        grid=(b, nt, N2 // s),
        in_specs=[
            pl.BlockSpec((h, 2 * h), lambda bb, t, j: (0, 0)),
            pl.BlockSpec((None, None, h, 2, s, LANES), lambda bb, t, j: (bb, t, 0, 0, j, 0)),
            pl.BlockSpec(nat, lambda bb, t, j: (bb, gate_group, t, 0, j, 0)),
            v_spec,
            pl.BlockSpec((None, None, 1, LANES), lambda bb, t, j: (order, t, 0, 0)),
        ],
        out_specs=out_spec,
        out_shape=out_shape,
        compiler_params=_cparams("arbitrary", "arbitrary", "arbitrary"),
        name="conv_stage3",
    )(f3i_bf, b6, gate6, v, skip4)


def _chdft_kernel(u_ref, fc_ref, w_ref, o_ref):
    gd = FNET_GROUP_DIM
    tpg = gd // LANES
    fc = fc_ref[...]
    for g in range(FNET_GROUPS):
        u = jnp.concatenate([u_ref[g * tpg + t] for t in range(tpg)], axis=1).astype(BF16)
        z = jnp.dot(u, fc, preferred_element_type=F32)
        w = w_ref[g]
        zr = jnp.dot(z[:, :gd].astype(BF16), w, preferred_element_type=F32)
        zi = jnp.dot(z[:, gd:].astype(BF16), w, preferred_element_type=F32)
        for t in range(tpg):
            o_ref[0, g * tpg + t] = zr[:, t * LANES:(t + 1) * LANES]
            o_ref[1, g * tpg + t] = zi[:, t * LANES:(t + 1) * LANES]


def _chdft(p, fc_bf, fnet_w_bf, *, tm):
    b, _, nt, L, _ = p.shape
    gd = FNET_GROUP_DIM
    return pl.pallas_call(
        _chdft_kernel,
        grid=(b, L // tm),
        in_specs=[
            pl.BlockSpec((None, None, nt, tm, LANES), lambda bb, i: (bb, 1, 0, i, 0)),
            pl.BlockSpec((gd, 2 * gd), lambda bb, i: (0, 0)),
            pl.BlockSpec((FNET_GROUPS, gd, gd), lambda bb, i: (0, 0, 0)),
        ],
        out_specs=pl.BlockSpec((None, 2, nt, tm, LANES), lambda bb, i: (bb, 0, 0, i, 0)),
        out_shape=jax.ShapeDtypeStruct((b, 2, nt, L, LANES), F32),
        compiler_params=_cparams("arbitrary", "arbitrary"),
        name="fnet_chdft",
    )(p, fc_bf, fnet_w_bf)


def _fnet2_kernel(re_ref, im_ref, f2r_ref, f2i_ref, twr_ref, twi_ref, b_ref, o_ref, *, kc):
    f2r = f2r_ref[...]
    f2i = f2i_ref[...]
    for q in range(kc):
        twr = twr_ref[q:q + 1, :]
        twi = twi_ref[q:q + 1, :]
        g = jnp.concatenate([f2r * twr - f2i * twi, -(f2r * twi + f2i * twr)], axis=1).astype(BF16)
        fg = jnp.dot(g, _load_a(re_ref, im_ref, q, kc), preferred_element_type=F32)
        for t in range(o_ref.shape[0]):
            o_ref[t, :, q:q + 1, :] = (fg[:, t * LANES:(t + 1) * LANES] + b_ref[t]).reshape(N2, 1, LANES)


def _fnet2(a1, f2r, f2i, twr, twi, fnet_b, *, kc, ntb):
    b, nt, _, m, _ = a1.shape
    n1 = m // 2
    nb = n1 // kc
    kern = functools.partial(_fnet2_kernel, kc=kc)
    return pl.pallas_call(
        kern,
        grid=(b, nt // ntb, nb),
        in_specs=[
            pl.BlockSpec((None, ntb, N2, kc, LANES), lambda bb, t, k: (bb, t, 0, k, 0)),
            pl.BlockSpec((None, ntb, N2, kc, LANES), lambda bb, t, k: (bb, t, 0, nb + k, 0)),
            pl.BlockSpec((N2, N2), lambda bb, t, k: (0, 0)),
            pl.BlockSpec((N2, N2), lambda bb, t, k: (0, 0)),
            pl.BlockSpec((kc, N2), lambda bb, t, k: (k, 0)),
            pl.BlockSpec((kc, N2), lambda bb, t, k: (k, 0)),
            pl.BlockSpec((ntb, 1, LANES), lambda bb, t, k: (t, 0, 0)),
        ],
        out_specs=pl.BlockSpec((None, ntb, N2, kc, LANES), lambda bb, t, k: (bb, t, 0, k, 0)),
        out_shape=jax.ShapeDtypeStruct((b, nt, N2, n1, LANES), F32),
        compiler_params=_cparams("arbitrary", "arbitrary", "arbitrary"),
        name="fnet_stage2",
    )(a1, a1, f2r, f2i, twr, twi, fnet_b.reshape(nt, 1, LANES))


def _outproj_kernel(zz_ref, zhy_ref, yg_ref, zfn_ref, x_ref, nhy_ref, nfn_ref, w_ref, fin_ref,
                    o_ref, cat_ref):
    def silu(z):
        return z / (1.0 + jnp.exp(-z))

    def branch(a_ref, z_ref, g_ref, col0):
        nt = a_ref.shape[0]
        a = [a_ref[t] * silu(z_ref[t].astype(F32)) for t in range(nt)]
        ss = sum(jnp.sum(at * at, axis=-1, keepdims=True) for at in a)
        rs = lax.rsqrt(ss * (1.0 / (nt * LANES)) + EPS)
        for t in range(nt):
            lo = col0 + t * LANES
            cat_ref[:, lo:lo + LANES] = (a[t] * rs * g_ref[:, t * LANES:(t + 1) * LANES]).astype(BF16)

    branch(zz_ref, zhy_ref, nhy_ref, 0)
    branch(yg_ref, zfn_ref, nfn_ref, zz_ref.shape[0] * LANES)
    y = x_ref[...] + jnp.dot(cat_ref[...], w_ref[...], preferred_element_type=F32)
    ms = jnp.mean(y * y, axis=-1, keepdims=True)
    o_ref[...] = y * lax.rsqrt(ms + EPS) * fin_ref[...]


def _outproj(zz, p, yg, x, norm_hy, norm_fn, w_out_bf, final_norm, *, tm):
    b, L, d = x.shape
    c = CH
    vecc = lambda: pl.BlockSpec((1, c), lambda bb, i: (0, 0))
    tiles = lambda: pl.BlockSpec((None, NT, tm, LANES), lambda bb, i: (bb, 0, i, 0))
    group = lambda g: pl.BlockSpec((None, None, NT, tm, LANES), lambda bb, i: (bb, g, 0, i, 0))
    return pl.pallas_call(
        _outproj_kernel,
        grid=(b, L // tm),
        in_specs=[
            tiles(), group(0), tiles(), group(2),
            pl.BlockSpec((None, tm, d), lambda bb, i: (bb, i, 0)),
            vecc(), vecc(),
            pl.BlockSpec((2 * c, d), lambda bb, i: (0, 0)),
            pl.BlockSpec((1, d), lambda bb, i: (0, 0)),
        ],
        out_specs=pl.BlockSpec((None, tm, d), lambda bb, i: (bb, i, 0)),
        out_shape=jax.ShapeDtypeStruct((b, L, d), F32),
        scratch_shapes=[pltpu.VMEM((tm, 2 * c), BF16)],
        compiler_params=_cparams("arbitrary", "arbitrary"),
        name="outproj",
    )(zz, p, yg, p, x, norm_hy.reshape(1, c), norm_fn.reshape(1, c), w_out_bf, final_norm.reshape(1, d))


def _trunk(x, norm_g, w_in_bf, conv_w, conv_b, w1p, b1, w2, b2, w3, b3, w4, freq, skip,
           fnet_w_bf, fnet_b, norm_hy, norm_fn, w_out_bf, final_norm):
    b, L, d = x.shape
    n = 2 * L
    h = n // N2 // 2
    n1f = L // N2
    s = min(N2, 8192 // h)
    kc = 8
    ntb = 4

    f1, f3, twr, twi = (jnp.asarray(t) for t in _conv_tables(L))
    f2r, f2i = (jnp.asarray(t) for t in _f2_tables())
    fn1, ftwr, ftwi, fc = (jnp.asarray(t) for t in _fnet_tables(L))
    f1_bf, f3i_bf, fn1_bf, fc_bf = (t.astype(BF16) for t in (f1, f3, fn1, fc))

    p, pz = _inproj(x, norm_g, w_in_bf, conv_w, conv_b, tm=min(1024, L))
    p6 = p.reshape(b, 3, NT, h, N2, LANES)
    nat = (None, None, None, h, s, LANES)

    filt = _filters(L, w1p, b1, w2, b2, w3, b3, freq, w4, tr=min(512, L))
    a1f = _s1(f1_bf, filt.reshape(1, 1, 4 * NT, h, N2, LANES), nat, lambda bb, t, j: (0, 0, t, 0, j, 0),
              nb=1, nt=4 * NT, rows=h, s=s, strided=True, name="filter_stage1")
    kspec = _filt2(a1f, f2r, f2i, twr, twi, kc=kc, ntb=2)

    skip4 = skip.reshape(2, NT, 1, LANES)

    a1 = _s1(f1_bf, p6, nat, lambda bb, t, j: (bb, 2, t, 0, j, 0), nb=b, nt=NT, rows=h, s=s, strided=True,
             name="conv_stage1")
    bsp = _conv2(a1, f2r, f2i, twr, twi, kspec, 0, kc=kc, ntb=ntb)
    zz1 = _s3(f3i_bf, bsp, p6, 0, p6, 2, skip4, 0, s=s, out_natural=False)
    a1 = _s1(f1_bf, zz1, (None, None, s, h, LANES), lambda bb, t, j: (bb, t, j, 0, 0), nb=b, nt=NT, rows=h, s=s,
             strided=False, name="conv_stage1")
    bsp = _conv2(a1, f2r, f2i, twr, twi, kspec, 1, kc=kc, ntb=ntb)
    zz2 = _s3(f3i_bf, bsp, p6, 1, zz1, None, skip4, 1, s=s, out_natural=True)

    z = _chdft(pz, fc_bf, fnet_w_bf, tm=min(512, L))
    sf = min(N2, 4096 // (2 * n1f))
    a1z = _s1(fn1_bf, z.reshape(b, 2, NT, n1f, N2, LANES), (None, 2, None, n1f, sf, LANES),
              lambda bb, t, j: (bb, 0, t, 0, j, 0), nb=b, nt=NT, rows=2 * n1f, s=sf, strided=True,
              name="fnet_stage1")
    yg = _fnet2(a1z, f2r, f2i, ftwr, ftwi, fnet_b, kc=kc, ntb=ntb)

    return _outproj(zz2.reshape(b, NT, L, LANES), pz, yg.reshape(b, NT, L, LANES), x, norm_hy, norm_fn,
                    w_out_bf, final_norm, tm=min(512, L))


def kernel(x_prompt, x_sample, norm_g, w_in, conv_w, conv_b, filt_w1, filt_b1, filt_w2, filt_b2,
           filt_w3, filt_b3, filt_w4, filt_freq, filt_skip, fnet_w, fnet_b, norm_hy, norm_fn,
           w_out, final_norm):
    l = 0
    w1p = jnp.zeros((FILTER_HIDDEN, FILTER_HIDDEN), F32).at[:FILTER_EMB].set(filt_w1[l])
    args = (norm_g[l], w_in[l].astype(BF16), conv_w[l], conv_b[l], w1p, filt_b1[l], filt_w2[l],
            filt_b2[l], filt_w3[l], filt_b3[l], filt_w4[l], filt_freq[l], filt_skip[l],
            fnet_w[l].astype(BF16), fnet_b[l], norm_hy[l], norm_fn[l], w_out[l].astype(BF16),
            final_norm)
    return (_trunk(x_prompt, *args), _trunk(x_sample, *args))
```
